```python
import math
import jax
import jax.numpy as jnp
from jax import lax
import numpy as np

D_MODEL = 2048
BATCH = 4
SEQ = 2048
DEPTH = 4
DEC_BATCH = 8
DEC_SEQ = 4
PAST_LEN = 16384
PAGE_SIZE = 128

N_EVEN = (DEPTH + 1) // 2
N_ODD = DEPTH // 2
Q_BLOCK = 128
EPS = 1e-6
HD_SB = 128
H_SB = D_MODEL // (2 * HD_SB)
DK_RET = 128
DV_RET = 256
H_RET = D_MODEL // (2 * DV_RET)
RET_CHUNK = 128
RET_THETA = 10000.0
HD_DIFF = 64
DV_DIFF = 2 * HD_DIFF
H_DIFF = D_MODEL // DV_DIFF
ROPE_THETA = 500000.0
ROT_DIM = HD_DIFF // 4
EVEN_IN = 3 * H_SB * HD_SB + H_RET * (2 * DK_RET + 2 * DV_RET)
EVEN_MIX = H_SB * HD_SB + H_RET * DV_RET
ODD_IN = H_DIFF * (4 * HD_DIFF + DV_DIFF)
ODD_MIX = H_DIFF * DV_DIFF
N_GROUPS = 4
EXP_PER_GROUP = 4
N_EXPERTS = N_GROUPS * EXP_PER_GROUP
TOP_K_IN_GROUP = 2
D_EXPERT = 512
ADA_MOD = 6

kernel_name = 'hybrid_sb_ret_diff_hmoe_step'


def rmsnorm(x, g):
    x32 = x.astype(jnp.float32)
    y = x32 * lax.rsqrt(jnp.mean(x32 * x32, axis=-1, keepdims=True) + EPS)
    return (y * g.astype(jnp.float32)).astype(x.dtype)


def head_layernorm(o, g):
    o32 = o.astype(jnp.float32)
    mu = jnp.mean(o32, axis=-1, keepdims=True)
    var = jnp.mean(jnp.square(o32 - mu), axis=-1, keepdims=True)
    return ((o32 - mu) * lax.rsqrt(var + EPS) * g.astype(jnp.float32)).astype(o.dtype)


def ada(c, w, b):
    m = (jax.nn.silu(c) @ w + b)[:, None, :]
    return [m[..., i * D_MODEL:(i + 1) * D_MODEL] for i in range(ADA_MOD)]


def modulate(x, g, shift, scale):
    return rmsnorm(x, g) * (1 + scale) + shift


def split_cols(p, sizes):
    out, start = [], 0
    for s in sizes:
        out.append(p[..., start:start + s])
        start += s
    return out


def rope(x, pos, rot_dim, theta):
    half = rot_dim // 2
    inv = jnp.float32(theta) ** (-jnp.arange(half, dtype=jnp.float32) / half)
    ang = pos.astype(jnp.float32)[:, None] * inv[None, :]
    shp = (pos.shape[0],) + (1,) * (x.ndim - 3) + (half,)
    cos = jnp.cos(ang).reshape(shp)
    sin = jnp.sin(ang).reshape(shp)
    x32 = x.astype(jnp.float32)
    x1 = x32[..., :half]
    x2 = x32[..., half:rot_dim]
    out = jnp.concatenate([x1 * cos - x2 * sin, x1 * sin + x2 * cos, x32[..., rot_dim:]], axis=-1)
    return out.astype(x.dtype)


def gather_pages(pool, page_table):
    g = pool[page_table]
    return g.reshape(page_table.shape[0], page_table.shape[1] * pool.shape[1], *pool.shape[2:])


def sweep_query_blocks(fn, q, q_pos):
    B, S = q.shape[0], q.shape[1]
    if S <= Q_BLOCK:
        return fn(q, q_pos)
    nb = S // Q_BLOCK
    qb = jnp.moveaxis(q.reshape(B, nb, Q_BLOCK, *q.shape[2:]), 1, 0)
    pb = q_pos.reshape(nb, Q_BLOCK)
    out = lax.map(lambda a: fn(a[0], a[1]), (qb, pb))
    return jnp.moveaxis(out, 0, 1).reshape(B, S, *out.shape[3:])


def sb_block(qi, qpos, k, v, kpos):
    z = jnp.einsum('bqhd,bkhd->bhqk', qi, k).astype(jnp.float32) * (HD_SB ** -0.5)
    valid = kpos[None, :] < qpos[:, None]
    log_keep = jnp.where(valid, jax.nn.log_sigmoid(-z), 0.0)
    after = lax.cumsum(log_keep, axis=3, reverse=True) - log_keep
    a = jnp.where(valid, jnp.exp(jax.nn.log_sigmoid(z) + after), 0.0)
    return jnp.einsum('bhqk,bkhd->bqhd', a.astype(v.dtype), v)


def diff_block(qi, qpos, k, v, kpos, lam):
    s = jnp.einsum('bqhcd,bkhcd->bchqk', qi, k).astype(jnp.float32) * (HD_DIFF ** -0.5)
    causal = kpos[None, :] <= qpos[:, None]
    p = jax.nn.softmax(jnp.where(causal, s, -jnp.inf), axis=-1)
    a = p[:, 0] - lam * p[:, 1]
    return jnp.einsum('bhqk,bkhd->bqhd', a.astype(v.dtype), v)


def retention(q, k, v, state0):
    B, S, H, dk = q.shape
    dv = v.shape[-1]
    C = min(RET_CHUNK, S)
    nc = S // C
    log_g = jnp.log1p(-(2.0 ** (-5.0 - jnp.arange(H, dtype=jnp.float32))))
    qc = q.astype(jnp.float32).reshape(B, nc, C, H, dk)
    kc = k.astype(jnp.float32).reshape(B, nc, C, H, dk) * (dk ** -0.5)
    vc = v.astype(jnp.float32).reshape(B, nc, C, H, dv)
    i = jnp.arange(C, dtype=jnp.float32)
    diff = i[:, None] - i[None, :]
    decay = jnp.where(diff >= 0, jnp.exp(jnp.maximum(diff, 0.0)[None] * log_g[:, None, None]), 0.0)
    scores = jnp.einsum('bnihd,bnjhd->bnhij', qc, kc) * decay
    inner = jnp.einsum('bnhij,bnjhe->bnihe', scores, vc)
    k_dec = jnp.exp((C - 1 - i)[:, None] * log_g[None, :])
    kv = jnp.einsum('bnjhd,jh,bnjhe->nbhde', kc, k_dec, vc)
    g_chunk = jnp.exp(C * log_g)[None, :, None, None]

    def step(s, kv_n):
        return g_chunk * s + kv_n, s

    s_final, s_prev = lax.scan(step, state0.astype(jnp.float32), kv)
    q_dec = jnp.exp((i + 1)[:, None] * log_g[None, :])
    cross = jnp.einsum('bnihd,ih,nbhde->bnihe', qc, q_dec, s_prev)
    o = (inner + cross).reshape(B, S, H, dv)
    return o.astype(q.dtype), s_final.astype(state0.dtype)


def even_mixer(h, pos, w_in, w_out, gn_gain, state0, past_k=None, past_v=None):
    B, S, _ = h.shape
    q_sb, k_sb, v_sb, q_r, k_r, v_r, g_r = split_cols(
        h @ w_in, (H_SB * HD_SB,) * 3 + (H_RET * DK_RET,) * 2 + (H_RET * DV_RET,) * 2)
    q_sb = q_sb.reshape(B, S, H_SB, HD_SB)
    k_sb = k_sb.reshape(B, S, H_SB, HD_SB)
    v_sb = v_sb.reshape(B, S, H_SB, HD_SB)
    if past_k is None:
        k_all, v_all, k_pos = k_sb, v_sb, pos
    else:
        k_all = jnp.concatenate([past_k, k_sb], axis=1)
        v_all = jnp.concatenate([past_v, v_sb], axis=1)
        k_pos = jnp.concatenate([jnp.arange(past_k.shape[1]), pos])
    o_sb = sweep_query_blocks(lambda qi, qp: sb_block(qi, qp, k_all, v_all, k_pos), q_sb, pos)
    q_r = rope(q_r.reshape(B, S, H_RET, DK_RET), pos, DK_RET, RET_THETA)
    k_r = rope(k_r.reshape(B, S, H_RET, DK_RET), pos, DK_RET, RET_THETA)
    o_r, s_new = retention(q_r, k_r, v_r.reshape(B, S, H_RET, DV_RET), state0)
    o_r = head_layernorm(o_r, gn_gain.reshape(H_RET, DV_RET)) * jax.nn.silu(g_r.reshape(B, S, H_RET, DV_RET))
    mix = jnp.concatenate([o_sb.reshape(B, S, -1), o_r.reshape(B, S, -1)], axis=-1)
    return mix @ w_out, k_sb, v_sb, s_new


def odd_mixer(h, pos, w_in, w_out, qk_g, lam_vec, sub_g, lam_init, past_k=None, past_v=None):
    B, S, _ = h.shape
    q, k, v = split_cols(h @ w_in, (H_DIFF * 2 * HD_DIFF,) * 2 + (H_DIFF * DV_DIFF,))
    q = rope(rmsnorm(q.reshape(B, S, H_DIFF, 2, HD_DIFF), qk_g[0]), pos, ROT_DIM, ROPE_THETA)
    k = rope(rmsnorm(k.reshape(B, S, H_DIFF, 2, HD_DIFF), qk_g[1]), pos, ROT_DIM, ROPE_THETA)
    v = v.reshape(B, S, H_DIFF, DV_DIFF)
    k_rows = k.reshape(B, S, H_DIFF, 2 * HD_DIFF)
    if past_k is None:
        k_all, v_all, k_pos = k, v, pos
    else:
        L = past_k.shape[1]
        k_all = jnp.concatenate([past_k.reshape(B, L, H_DIFF, 2, HD_DIFF), k], axis=1)
        v_all = jnp.concatenate([past_v, v], axis=1)
        k_pos = jnp.concatenate([jnp.arange(L), pos])
    lv = lam_vec.astype(jnp.float32)
    lam = jnp.exp(jnp.sum(lv[0] * lv[1])) - jnp.exp(jnp.sum(lv[2] * lv[3])) + lam_init
    o = sweep_query_blocks(lambda qi, qp: diff_block(qi, qp, k_all, v_all, k_pos, lam), q, pos)
    o = rmsnorm(o, sub_g) * (1.0 - lam_init)
    return o.reshape(B, S, -1) @ w_out, k_rows, v


def hier_moe(h, w_rg, b_rg, w_re, b_re, w_in, w_out):
    B, S, D = h.shape
    x = h.reshape(-1, D)
    pg = jax.nn.softmax((x @ w_rg + b_rg).astype(jnp.float32), axis=-1)
    g_top, g_idx = lax.top_k(pg, 1)
    le = jnp.einsum('td,gde->tge', x, w_re) + b_re
    le_sel = jnp.take_along_axis(le, g_idx[:, :, None], axis=1)[:, 0]
    pe = jax.nn.softmax(le_sel.astype(jnp.float32), axis=-1)
    e_w, e_idx = lax.top_k(pe, TOP_K_IN_GROUP)
    e_w = e_w / jnp.sum(e_w, axis=-1, keepdims=True) * g_top
    eid = g_idx * EXP_PER_GROUP + e_idx
    gates = jnp.sum(jax.nn.one_hot(eid, N_EXPERTS, dtype=jnp.float32) * e_w[..., None], axis=1)
    hid = jnp.einsum('td,edf->tef', x, w_in)
    act = jax.nn.silu(hid[..., D_EXPERT:]) * hid[..., :D_EXPERT] * gates[..., None].astype(x.dtype)
    y = jnp.einsum('tef,efd->td', act, w_out)
    return y.reshape(B, S, D)


def setup_inputs(seed: int = 0) -> dict:
    key = jax.random.key(seed)
    ks = iter(jax.random.split(key, 40))
    f32 = jnp.float32
    n_pages = PAST_LEN // PAGE_SIZE
    n_used = DEC_BATCH * n_pages
    n_pool = n_used + n_used // 4

    def nrm(shape, scale):
        return jax.random.normal(next(ks), shape, f32) * scale

    x_prompt = nrm((BATCH, SEQ, D_MODEL), 1.0)
    x_sample = nrm((DEC_BATCH, DEC_SEQ, D_MODEL), 1.0)
    cache_sb_k = nrm((N_EVEN, n_pool, PAGE_SIZE, H_SB, HD_SB), 1.0)
    cache_sb_v = nrm((N_EVEN, n_pool, PAGE_SIZE, H_SB, HD_SB), 1.0)
    state_ret = nrm((N_EVEN, DEC_BATCH, H_RET, DK_RET, DV_RET), 0.1)
    cache_diff_k = nrm((N_ODD, n_pool, PAGE_SIZE, H_DIFF, 2 * HD_DIFF), 1.0)
    cache_diff_v = nrm((N_ODD, n_pool, PAGE_SIZE, H_DIFF, DV_DIFF), 1.0)
    page_table = jax.random.permutation(next(ks), n_pool)[:n_used].reshape(DEC_BATCH, n_pages).astype(jnp.int32)
    c_prompt = nrm((BATCH, D_MODEL), 1.0)
    c_sample = nrm((DEC_BATCH, D_MODEL), 1.0)
    w_ada = nrm((DEPTH, D_MODEL, ADA_MOD * D_MODEL), 0.5 * D_MODEL ** -0.5)
    b_ada = nrm((DEPTH, ADA_MOD * D_MODEL), 0.02)
    norm_mix = 1.0 + nrm((DEPTH, D_MODEL), 0.02)
    norm_ffn = 1.0 + nrm((DEPTH, D_MODEL), 0.02)
    w_in_even = nrm((N_EVEN, D_MODEL, EVEN_IN), D_MODEL ** -0.5)
    w_out_even = nrm((N_EVEN, EVEN_MIX, D_MODEL), EVEN_MIX ** -0.5)
    ret_gn_gain = 1.0 + nrm((N_EVEN, H_RET * DV_RET), 0.02)
    w_in_odd = nrm((N_ODD, D_MODEL, ODD_IN), D_MODEL ** -0.5)
    w_out_odd = nrm((N_ODD, ODD_MIX, D_MODEL), ODD_MIX ** -0.5)
    qk_gain = 1.0 + nrm((N_ODD, 2, HD_DIFF), 0.02)
    diff_lambda = nrm((N_ODD, 4, HD_DIFF), 0.1)
    diff_subln = 1.0 + nrm((N_ODD, DV_DIFF), 0.02)
    w_router_group = nrm((DEPTH, D_MODEL, N_GROUPS), D_MODEL ** -0.5)
    b_router_group = nrm((DEPTH, N_GROUPS), 0.01)
    w_router_expert = nrm((DEPTH, N_GROUPS, D_MODEL, EXP_PER_GROUP), D_MODEL ** -0.5)
    b_router_expert = nrm((DEPTH, N_GROUPS, EXP_PER_GROUP), 0.01)
    w_expert_in = nrm((DEPTH, N_EXPERTS, D_MODEL, 2 * D_EXPERT), D_MODEL ** -0.5)
    w_expert_out = nrm((DEPTH, N_EXPERTS, D_EXPERT, D_MODEL), D_EXPERT ** -0.5)
    return {'x_prompt': x_prompt, 'x_sample': x_sample,
            'cache_sb_k': cache_sb_k, 'cache_sb_v': cache_sb_v, 'state_ret': state_ret,
            'cache_diff_k': cache_diff_k, 'cache_diff_v': cache_diff_v, 'page_table': page_table,
            'c_prompt': c_prompt, 'c_sample': c_sample,
            'w_ada': w_ada, 'b_ada': b_ada, 'norm_mix': norm_mix, 'norm_ffn': norm_ffn,
            'w_in_even': w_in_even, 'w_out_even': w_out_even, 'ret_gn_gain': ret_gn_gain,
            'w_in_odd': w_in_odd, 'w_out_odd': w_out_odd, 'qk_gain': qk_gain,
            'diff_lambda': diff_lambda, 'diff_subln': diff_subln,
            'w_router_group': w_router_group, 'b_router_group': b_router_group,
            'w_router_expert': w_router_expert, 'b_router_expert': b_router_expert,
            'w_expert_in': w_expert_in, 'w_expert_out': w_expert_out}


def reference(x_prompt, x_sample, cache_sb_k, cache_sb_v, state_ret, cache_diff_k, cache_diff_v,
              page_table, c_prompt, c_sample, w_ada, b_ada, norm_mix, norm_ffn,
              w_in_even, w_out_even, ret_gn_gain, w_in_odd, w_out_odd, qk_gain,
              diff_lambda, diff_subln, w_router_group, b_router_group,
              w_router_expert, b_router_expert, w_expert_in, w_expert_out):
    S = x_prompt.shape[1]
    T = x_sample.shape[1]
    past_len = page_table.shape[1] * cache_sb_k.shape[2]
    pos_p = jnp.arange(S)
    pos_s = past_len + jnp.arange(T)
    xp, xs = x_prompt, x_sample
    sbk_p, sbv_p, sbk_s, sbv_s, ret_p, ret_s = [], [], [], [], [], []
    dk_p, dv_p, dk_s, dv_s = [], [], [], []
    for li in range(DEPTH):
        mp = ada(c_prompt, w_ada[li], b_ada[li])
        ms = ada(c_sample, w_ada[li], b_ada[li])
        hp = modulate(xp, norm_mix[li], mp[0], mp[1])
        hs = modulate(xs, norm_mix[li], ms[0], ms[1])
        if li % 2 == 0:
            e = li // 2
            zero_state = jnp.zeros((xp.shape[0], H_RET, DK_RET, DV_RET), x_prompt.dtype)
            op, kp, vp, sp = even_mixer(hp, pos_p, w_in_even[e], w_out_even[e], ret_gn_gain[e], zero_state)
            os_, ks_, vs_, ss_ = even_mixer(hs, pos_s, w_in_even[e], w_out_even[e], ret_gn_gain[e], state_ret[e],
                                            gather_pages(cache_sb_k[e], page_table),
                                            gather_pages(cache_sb_v[e], page_table))
            sbk_p.append(kp)
            sbv_p.append(vp)
            sbk_s.append(ks_)
            sbv_s.append(vs_)
            ret_p.append(sp)
            ret_s.append(ss_)
        else:
            o = li // 2
            lam_init = 0.8 - 0.6 * math.exp(-0.3 * li)
            op, kp, vp = odd_mixer(hp, pos_p, w_in_odd[o], w_out_odd[o], qk_gain[o], diff_lambda[o],
                                   diff_subln[o], lam_init)
            os_, ks_, vs_ = odd_mixer(hs, pos_s, w_in_odd[o], w_out_odd[o], qk_gain[o], diff_lambda[o],
                                      diff_subln[o], lam_init,
                                      gather_pages(cache_diff_k[o], page_table),
                                      gather_pages(cache_diff_v[o], page_table))
            dk_p.append(kp)
            dv_p.append(vp)
            dk_s.append(ks_)
            dv_s.append(vs_)
        xp = xp + mp[2] * op
        xs = xs + ms[2] * os_
        hp = modulate(xp, norm_ffn[li], mp[3], mp[4])
        hs = modulate(xs, norm_ffn[li], ms[3], ms[4])
        moe_args = (w_router_group[li], b_router_group[li], w_router_expert[li], b_router_expert[li],
                    w_expert_in[li], w_expert_out[li])
        xp = xp + mp[5] * hier_moe(hp, *moe_args)
        xs = xs + ms[5] * hier_moe(hs, *moe_args)
    return (xp, xs, jnp.stack(sbk_p), jnp.stack(sbv_p), jnp.stack(sbk_s), jnp.stack(sbv_s),
            jnp.stack(ret_p), jnp.stack(ret_s), jnp.stack(dk_p), jnp.stack(dv_p),
            jnp.stack(dk_s), jnp.stack(dv_s))
```

```python
import functools
import math

import jax
import jax.numpy as jnp
import numpy as np
from jax import lax
from jax.experimental import pallas as pl
from jax.experimental.pallas import tpu as pltpu

D_MODEL = 2048
BATCH = 4
SEQ = 2048
DEPTH = 4
DEC_BATCH = 8
DEC_SEQ = 4
PAGE_SIZE = 128
EPS = 1e-6
HD_SB = 128
DK_RET = 128
DV_RET = 256
RET_THETA = 10000.0
HD_DIFF = 64
DV_DIFF = 2 * HD_DIFF
ROPE_THETA = 500000.0
ROT_DIM = HD_DIFF // 4
N_GROUPS = 4
EXP_PER_GROUP = 4
N_EXPERTS = N_GROUPS * EXP_PER_GROUP
D_EXPERT = 512
ADA_MOD = 6

F32 = jnp.float32
BF16 = jnp.bfloat16

LANES = 128
SUBLANES = 8
MOD_ROWS = 16
SROWS = SUBLANES
ATT_BLK = 128
EXP_TILE = 256
SB_EXIT = -104.0
VMEM_LIMIT = 56 * 2**20


def _cp(sem, vmem=VMEM_LIMIT):
    return pltpu.CompilerParams(dimension_semantics=sem, vmem_limit_bytes=vmem)


def _dot(a, b):
    return jnp.dot(a, b, preferred_element_type=F32)


def _dot_nt(a, b):
    return lax.dot_general(a, b, (((1,), (1,)), ((), ())), preferred_element_type=F32)


def _split_bf16(x):
    hi = x.astype(BF16)
    lo = (x - hi.astype(F32)).astype(BF16)
    return hi, lo


def _modulate(x, g, shift, scale):
    y = x * lax.rsqrt(jnp.mean(x * x, axis=-1, keepdims=True) + EPS) * g
    return y * (1.0 + scale) + shift


def _sample_row_valid(rows):
    r = lax.broadcasted_iota(jnp.int32, (rows, 1), 0)
    return (r & (SROWS - 1)) < DEC_SEQ


def _ada_body(c_ref, w_ref, b_ref, o_ref):
    c = c_ref[...]
    a = (c * jax.nn.sigmoid(c)).astype(BF16)
    o_ref[0, 0] = _dot(a, w_ref[0].astype(BF16)) + b_ref[0]


def _ada(c_all, w_ada, b_ada):
    depth, d, _ = w_ada.shape
    tn = min(1024, d)
    nj = d // tn
    return pl.pallas_call(
        _ada_body,
        grid=(depth, ADA_MOD, nj),
        in_specs=[
            pl.BlockSpec((MOD_ROWS, d), lambda l, k, j: (0, 0)),
            pl.BlockSpec((1, d, tn), lambda l, k, j: (l, 0, k * nj + j)),
            pl.BlockSpec((1, 1, tn), lambda l, k, j: (l, 0, k * nj + j)),
        ],
        out_specs=pl.BlockSpec((1, 1, MOD_ROWS, tn), lambda l, k, j: (l, k, 0, j)),
        out_shape=jax.ShapeDtypeStruct((depth, ADA_MOD, MOD_ROWS, d), F32),
        compiler_params=_cp(("arbitrary",) * 3),
        name="ada",
    )(c_all, w_ada, b_ada.reshape(depth, 1, -1))


def _prompt_tile():
    return min(1024, SEQ)


def _mod_prompt_tile(x_ref, h_ref, g, sh, sc, rows):
    chunk = min(256, rows)

    def body(r, c):
        sl = pl.ds(pl.multiple_of(r * chunk, chunk), chunk)
        h_ref[sl, :] = _modulate(x_ref[sl, :], g, sh, sc).astype(h_ref.dtype)
        return c

    lax.fori_loop(0, rows // chunk, body, 0)


def _mod_sample_tile(xs_ref, g, shs_ref, scs_ref):
    h = _modulate(xs_ref[...], g, shs_ref[...], scs_ref[...])
    return jnp.where(_sample_row_valid(xs_ref.shape[0]), h, 0.0)


def _inproj_body(xp_ref, xs_ref, g_ref, sh_ref, sc_ref, shs_ref, scs_ref, w_ref,
                 op_ref, os_ref, hp_scr, hs_scr, *, n_p, tiles_per_batch):
    i = pl.program_id(0)
    j = pl.program_id(1)
    tm = xp_ref.shape[0]

    @pl.when(jnp.logical_and(i < n_p, j == 0))
    def _():
        b = i // tiles_per_batch
        _mod_prompt_tile(xp_ref, hp_scr, g_ref[...], sh_ref[0, 0, pl.ds(b, 1), :],
                         sc_ref[0, 0, pl.ds(b, 1), :], tm)

    @pl.when(jnp.logical_and(i == n_p, j == 0))
    def _():
        hs_scr[...] = _mod_sample_tile(xs_ref, g_ref[...], shs_ref, scs_ref).astype(BF16)

    @pl.when(i < n_p)
    def _():
        chunk = min(512, tm)

        def body(r, c):
            sl = pl.ds(pl.multiple_of(r * chunk, chunk), chunk)
            op_ref[0, sl, :] = _dot(hp_scr[sl, :], w_ref[...])
            return c

        lax.fori_loop(0, tm // chunk, body, 0)

    @pl.when(i == n_p)
    def _():
        os_ref[0] = _dot(hs_scr[...], w_ref[...])


def _inproj(xp, xs, g, mods, li, k_shift, k_scale, mod_s, w, group_w):
    tp, d = xp.shape
    sr = xs.shape[0]
    n = w.shape[1]
    tm = _prompt_tile()
    n_p = tp // tm
    tn = min(1024, group_w)
    nj = n // tn
    tpg = group_w // tn
    last = n_p - 1

    def p_out(i, j):
        jj = jnp.where(i == n_p, nj - 1, j)
        return (jj // tpg, jnp.minimum(i, last), jj % tpg)

    def s_out(i, j):
        jj = jnp.where(i == n_p, j, 0)
        return (jj // tpg, 0, jj % tpg)

    body = functools.partial(_inproj_body, n_p=n_p, tiles_per_batch=SEQ // tm)
    return pl.pallas_call(
        body,
        grid=(n_p + 1, nj),
        in_specs=[
            pl.BlockSpec((tm, d), lambda i, j: (jnp.minimum(i, last), 0)),
            pl.BlockSpec((sr, d), lambda i, j: (0, 0)),
            pl.BlockSpec((1, d), lambda i, j: (0, 0)),
            pl.BlockSpec((1, 1, MOD_ROWS, d), lambda i, j: (li, k_shift, 0, 0)),
            pl.BlockSpec((1, 1, MOD_ROWS, d), lambda i, j: (li, k_scale, 0, 0)),
            pl.BlockSpec((sr, d), lambda i, j: (0, 0)),
            pl.BlockSpec((sr, d), lambda i, j: (0, 0)),
            pl.BlockSpec((d, tn), lambda i, j: (0, j)),
        ],
        out_specs=[
            pl.BlockSpec((1, tm, tn), p_out),
            pl.BlockSpec((1, sr, tn), s_out),
        ],
        out_shape=[
            jax.ShapeDtypeStruct((n // group_w, tp, group_w), F32),
            jax.ShapeDtypeStruct((n // group_w, sr, group_w), F32),
        ],
        scratch_shapes=[pltpu.VMEM((tm, d), BF16), pltpu.VMEM((sr, d), BF16)],
        compiler_params=_cp(("arbitrary", "arbitrary")),
        name="inproj",
    )(xp, xs, g, mods, mods, mod_s[k_shift], mod_s[k_scale], w)


def _outproj_body(a1p_ref, a2p_ref, a1s_ref, a2s_ref, w1_ref, w2_ref, xp_ref, xs_ref,
                  gate_ref, gates_ref, op_ref, os_ref, *, n_p, tiles_per_batch):
    i = pl.program_id(0)
    tm = xp_ref.shape[0]

    @pl.when(i < n_p)
    def _():
        b = i // tiles_per_batch
        gate = gate_ref[0, 0, pl.ds(b, 1), :]
        chunk = min(512, tm)

        def body(r, c):
            sl = pl.ds(pl.multiple_of(r * chunk, chunk), chunk)
            y = _dot(a1p_ref[sl, :], w1_ref[...]) + _dot(a2p_ref[sl, :], w2_ref[...])
            op_ref[sl, :] = xp_ref[sl, :] + gate * y
            return c

        lax.fori_loop(0, tm // chunk, body, 0)

    @pl.when(i == n_p)
    def _():
        y = (_dot(a1s_ref[...].astype(BF16), w1_ref[...])
             + _dot(a2s_ref[...].astype(BF16), w2_ref[...]))
        os_ref[...] = xs_ref[...] + gates_ref[...] * y


def _outproj(ap, as_, w, xp, xs, mods, li, k_gate, mod_s):
    tp, d = xp.shape
    sr = xs.shape[0]
    tm = _prompt_tile()
    n_p = tp // tm
    tn = min(1024, d)
    nj = d // tn
    last = n_p - 1
    half = w.shape[0] // 2
    if isinstance(ap, tuple):
        a1p, a2p, a1s, a2s = ap[0], ap[1], as_[0], as_[1]
        c2 = 0
    else:
        a1p = a2p = ap
        a1s = a2s = as_
        c2 = 1

    def pin(c):
        return lambda i, j: (jnp.minimum(i, last), c)

    def p_out(i, j):
        return (jnp.minimum(i, last), jnp.where(i == n_p, nj - 1, j))

    def s_out(i, j):
        return (0, jnp.where(i == n_p, j, 0))

    body = functools.partial(_outproj_body, n_p=n_p, tiles_per_batch=SEQ // tm)
    return pl.pallas_call(
        body,
        grid=(n_p + 1, nj),
        in_specs=[
            pl.BlockSpec((tm, half), pin(0)),
            pl.BlockSpec((tm, half), pin(c2)),
            pl.BlockSpec((sr, half), lambda i, j: (0, 0)),
            pl.BlockSpec((sr, half), lambda i, j: (0, c2)),
            pl.BlockSpec((half, tn), lambda i, j: (0, j)),
            pl.BlockSpec((half, tn), lambda i, j: (1, j)),
            pl.BlockSpec((tm, tn), lambda i, j: (jnp.minimum(i, last), j)),
            pl.BlockSpec((sr, tn), lambda i, j: (0, j)),
            pl.BlockSpec((1, 1, MOD_ROWS, tn), lambda i, j: (li, k_gate, 0, j)),
            pl.BlockSpec((sr, tn), lambda i, j: (0, j)),
        ],
        out_specs=[pl.BlockSpec((tm, tn), p_out), pl.BlockSpec((sr, tn), s_out)],
        out_shape=[jax.ShapeDtypeStruct((tp, d), F32), jax.ShapeDtypeStruct((sr, d), F32)],
        compiler_params=_cp(("arbitrary", "arbitrary")),
        name="outproj",
    )(a1p, a2p, a1s, a2s, w, w, xp, xs, mods, mod_s[k_gate])


def _cumsum_mat(n):
    j = np.arange(n)[:, None]
    s = np.arange(n)[None, :]
    blk = np.concatenate([(j >= s).astype(np.float32), np.ones((n, n), np.float32)], axis=1)
    return jnp.asarray(np.concatenate([blk, blk], axis=0), dtype=BF16)


def _sb_block(qb, kblk, vblk, uu, carry, mask, scale):
    n = kblk.shape[0]
    z = _dot_nt(qb, kblk) * scale
    lk = -(jnp.maximum(z, 0.0) + jnp.log1p(jnp.exp(-jnp.abs(z))))
    if mask is not None:
        lk = jnp.where(mask, lk, 0.0)
    hi, lo = _split_bf16(lk)
    r = _dot(jnp.concatenate([hi, lo], axis=1), uu)
    a = jnp.exp(z + r[:, :n] + carry)
    if mask is not None:
        a = jnp.where(mask, a, 0.0)
    return carry + r[:, n:], _dot(a.astype(BF16), vblk)


def _sb_prompt_body(q_ref, k_ref, v_ref, uu_ref, o_ref, kb_scr, vb_scr, carry_scr, acc_scr,
                    *, nq, scale):
    blk = ATT_BLK
    kb_scr[...] = k_ref[0].astype(BF16)
    vb_scr[...] = v_ref[0].astype(BF16)
    row = lax.broadcasted_iota(jnp.int32, (blk, blk), 0)
    col = lax.broadcasted_iota(jnp.int32, (blk, blk), 1)
    strict = col < row
    uu = uu_ref[...]

    def q_body(qi, c):
        rows = pl.ds(pl.multiple_of(qi * blk, blk), blk)
        qb = q_ref[0, rows, :].astype(BF16)
        carry, o = _sb_block(qb, kb_scr[rows, :], vb_scr[rows, :], uu,
                             jnp.zeros((blk, blk), F32), strict, scale)
        carry_scr[...] = carry
        acc_scr[...] = o

        def cond(st):
            kb, mx = st
            return jnp.logical_and(kb >= 0, mx > SB_EXIT)

        def body(st):
            kb, _ = st
            rk = pl.ds(pl.multiple_of(kb * blk, blk), blk)
            carry, o = _sb_block(qb, kb_scr[rk, :], vb_scr[rk, :], uu, carry_scr[...], None, scale)
            carry_scr[...] = carry
            acc_scr[...] += o
            return kb - 1, jnp.max(carry)

        lax.while_loop(cond, body, (qi - 1, jnp.max(carry)))
        o_ref[rows, :] = acc_scr[...].astype(o_ref.dtype)
        return c

    lax.fori_loop(0, nq, q_body, 0)


def _sb_prompt(proj, uu):
    _, tp, width = proj.shape
    nb = tp // SEQ
    nh = width // HD_SB
    body = functools.partial(_sb_prompt_body, nq=SEQ // ATT_BLK, scale=HD_SB ** -0.5)
    spec = lambda g: pl.BlockSpec((1, SEQ, HD_SB), lambda b, h: (g, b, h))
    return pl.pallas_call(
        body,
        grid=(nb, nh),
        in_specs=[spec(0), spec(1), spec(2),
                  pl.BlockSpec((2 * ATT_BLK, 2 * ATT_BLK), lambda b, h: (0, 0))],
        out_specs=pl.BlockSpec((SEQ, HD_SB), lambda b, h: (b, h)),
        out_shape=jax.ShapeDtypeStruct((tp, width), BF16),
        scratch_shapes=[pltpu.VMEM((SEQ, HD_SB), BF16), pltpu.VMEM((SEQ, HD_SB), BF16),
                        pltpu.VMEM((ATT_BLK, ATT_BLK), F32), pltpu.VMEM((ATT_BLK, ATT_BLK), F32)],
        compiler_params=_cp(("arbitrary", "arbitrary")),
        name="sb_prompt",
    )(proj, proj, proj, uu)


def _sb_sample_body(pt_ref, q_ref, kn_ref, vn_ref, kc_ref, vc_ref, uu_ref, o_ref,
                    carry_scr, acc_scr, done_scr, *, n_heads, n_steps, scale):
    s = pl.program_id(1)
    blk = ATT_BLK
    uu = uu_ref[...]
    row = lax.broadcasted_iota(jnp.int32, (SROWS, blk), 0)
    col = lax.broadcasted_iota(jnp.int32, (SROWS, blk), 1)
    real_row = row < DEC_SEQ

    @pl.when(s == 0)
    def _():
        strict = col < row
        mx = jnp.full((1, 1), -jnp.inf, F32)
        for h in range(n_heads):
            cols = slice(h * HD_SB, (h + 1) * HD_SB)
            qb = q_ref[0, :, cols].astype(BF16)
            pad = jnp.zeros((blk - SROWS, HD_SB), F32)
            kblk = jnp.concatenate([kn_ref[0, :, cols], pad], axis=0).astype(BF16)
            vblk = jnp.concatenate([vn_ref[0, :, cols], pad], axis=0).astype(BF16)
            carry, o = _sb_block(qb, kblk, vblk, uu, jnp.zeros((SROWS, blk), F32), strict, scale)
            rows = slice(h * SROWS, (h + 1) * SROWS)
            carry_scr[rows, :] = carry
            acc_scr[rows, :] = o
            mx = jnp.maximum(mx, jnp.max(jnp.where(real_row, carry, -jnp.inf), keepdims=True))
        done_scr[0] = (jnp.max(mx) <= SB_EXIT).astype(jnp.int32)

    @pl.when(jnp.logical_and(s > 0, done_scr[0] == 0))
    def _():
        mx = jnp.full((1, 1), -jnp.inf, F32)
        for h in range(n_heads):
            cols = slice(h * HD_SB, (h + 1) * HD_SB)
            qb = q_ref[0, :, cols].astype(BF16)
            kblk = kc_ref[0, 0, pl.ds(h, PAGE_SIZE, stride=n_heads), :].astype(BF16)
            vblk = vc_ref[0, 0, pl.ds(h, PAGE_SIZE, stride=n_heads), :].astype(BF16)
            rows = slice(h * SROWS, (h + 1) * SROWS)
            carry, o = _sb_block(qb, kblk, vblk, uu, carry_scr[rows, :], None, scale)
            carry_scr[rows, :] = carry
            acc_scr[rows, :] += o
            mx = jnp.maximum(mx, jnp.max(jnp.where(real_row, carry, -jnp.inf), keepdims=True))
        done_scr[0] = (jnp.max(mx) <= SB_EXIT).astype(jnp.int32)

    @pl.when(s == n_steps - 1)
    def _():
        for h in range(n_heads):
            o_ref[:, h * HD_SB:(h + 1) * HD_SB] = acc_scr[h * SROWS:(h + 1) * SROWS, :]


def _sb_sample(proj_s, cache_k, cache_v, e, page_table, uu):
    _, sr, width = proj_s.shape
    nh = width // HD_SB
    nreq, n_pages = page_table.shape
    n_steps = n_pages + 1
    body = functools.partial(_sb_sample_body, n_heads=nh, n_steps=n_steps, scale=HD_SB ** -0.5)
    spec = lambda g: pl.BlockSpec((1, SROWS, width), lambda b, s, pt: (g, b, 0))

    def page(b, s, pt):
        return (e, pt[b, n_pages - jnp.maximum(s, 1)], 0, 0)

    return pl.pallas_call(
        body,
        grid_spec=pltpu.PrefetchScalarGridSpec(
            num_scalar_prefetch=1,
            grid=(nreq, n_steps),
            in_specs=[spec(0), spec(1), spec(2),
                      pl.BlockSpec((1, 1, PAGE_SIZE * nh, HD_SB), page),
                      pl.BlockSpec((1, 1, PAGE_SIZE * nh, HD_SB), page),
                      pl.BlockSpec((2 * ATT_BLK, 2 * ATT_BLK), lambda b, s, pt: (0, 0))],
            out_specs=pl.BlockSpec((SROWS, width), lambda b, s, pt: (b, 0)),
            scratch_shapes=[pltpu.VMEM((nh * SROWS, ATT_BLK), F32),
                            pltpu.VMEM((nh * SROWS, ATT_BLK), F32),
                            pltpu.SMEM((1,), jnp.int32)],
        ),
        out_shape=jax.ShapeDtypeStruct((sr, width), F32),
        compiler_params=_cp(("arbitrary", "arbitrary")),
        name="sb_sample",
    )(page_table, proj_s, proj_s, proj_s, cache_k, cache_v, uu)


def _ret_tables(pos, chunk, n_heads):
    half = DK_RET // 2
    inv = np.float32(RET_THETA) ** (-np.arange(half, dtype=np.float32) / half)
    ang = pos.astype(F32)[:, None] * jnp.asarray(inv)[None, :]
    cos = jnp.concatenate([jnp.cos(ang), jnp.cos(ang)], axis=1)
    sin = jnp.concatenate([-jnp.sin(ang), jnp.sin(ang)], axis=1)
    log_g = jnp.log1p(-(2.0 ** (-5.0 - jnp.arange(n_heads, dtype=F32))))
    i = jnp.arange(chunk, dtype=F32)
    return cos, sin, log_g, i


def _rope_half(x, cos, sin):
    return x * cos + pltpu.roll(x, x.shape[-1] // 2, axis=1) * sin


def _head_ln_gate(o, gain, g):
    mu = jnp.mean(o, axis=-1, keepdims=True)
    d = o - mu
    var = jnp.mean(d * d, axis=-1, keepdims=True)
    return d * lax.rsqrt(var + EPS) * gain * (g * jax.nn.sigmoid(g))


def _ret_prompt_body(q_ref, k_ref, v_ref, g_ref, cos_ref, sin_ref, dec_ref, qd_ref, kd_ref,
                     gc_ref, gain_ref, o_ref, s_ref, s_scr, *, n_chunks):
    c = ATT_BLK
    s_scr[...] = jnp.zeros_like(s_scr)
    decay = dec_ref[0]
    qd = qd_ref[0]
    kd = kd_ref[0]
    gc = gc_ref[0, 0:1, :1]
    gain = gain_ref[...]

    def body(n, carry):
        rows = pl.ds(pl.multiple_of(n * c, c), c)
        cos = cos_ref[rows, :]
        sin = sin_ref[rows, :]
        q = _rope_half(q_ref[0, rows, :], cos, sin)
        k = _rope_half(k_ref[0, rows, :], cos, sin) * (DK_RET ** -0.5)
        vb = v_ref[0, rows, :].astype(BF16)
        s_prev = s_scr[...]
        scores = _dot_nt(q.astype(BF16), k.astype(BF16)) * decay
        o = _dot(scores.astype(BF16), vb) + _dot((q * qd).astype(BF16), s_prev.astype(BF16))
        kv = _dot(jnp.transpose(k * kd).astype(BF16), vb)
        s_scr[...] = gc * s_prev + kv
        o_ref[rows, :] = _head_ln_gate(o, gain, g_ref[0, rows, :]).astype(o_ref.dtype)
        return carry

    lax.fori_loop(0, n_chunks, body, 0)
    s_ref[0, 0] = s_scr[...]


def _ret_prompt(proj, gain):
    _, tp, width = proj.shape
    nb = tp // SEQ
    nh = width // DV_RET
    c = ATT_BLK
    cos, sin, log_g, i = _ret_tables(jnp.arange(SEQ), c, nh)
    diff = i[:, None] - i[None, :]
    decay = jnp.where(diff >= 0, jnp.exp(jnp.maximum(diff, 0.0)[None] * log_g[:, None, None]), 0.0)
    ones = jnp.ones((1, 1, DK_RET), F32)
    qd = jnp.exp((i + 1)[None, :, None] * log_g[:, None, None]) * ones
    kd = jnp.exp((c - 1 - i)[None, :, None] * log_g[:, None, None]) * ones
    gc = jnp.exp(c * log_g)[:, None, None] * jnp.ones((1, SUBLANES, LANES), F32)
    body = functools.partial(_ret_prompt_body, n_chunks=SEQ // c)
    tab = lambda: pl.BlockSpec((SEQ, DK_RET), lambda b, h: (0, 0))
    per_head = lambda r, w: pl.BlockSpec((1, r, w), lambda b, h: (h, 0, 0))
    return pl.pallas_call(
        body,
        grid=(nb, nh),
        in_specs=[
            pl.BlockSpec((1, SEQ, DK_RET), lambda b, h: (3, b, h)),
            pl.BlockSpec((1, SEQ, DK_RET), lambda b, h: (3, b, nh + h)),
            pl.BlockSpec((1, SEQ, DV_RET), lambda b, h: (4, b, h)),
            pl.BlockSpec((1, SEQ, DV_RET), lambda b, h: (5, b, h)),
            tab(), tab(),
            per_head(c, c), per_head(c, DK_RET), per_head(c, DK_RET), per_head(SUBLANES, LANES),
            pl.BlockSpec((1, DV_RET), lambda b, h: (0, h)),
        ],
        out_specs=[pl.BlockSpec((SEQ, DV_RET), lambda b, h: (b, h)),
                   pl.BlockSpec((1, 1, DK_RET, DV_RET), lambda b, h: (b, h, 0, 0))],
        out_shape=[jax.ShapeDtypeStruct((tp, width), BF16),
                   jax.ShapeDtypeStruct((nb, nh, DK_RET, DV_RET), F32)],
        scratch_shapes=[pltpu.VMEM((DK_RET, DV_RET), F32)],
        compiler_params=_cp(("arbitrary", "arbitrary")),
        name="ret_prompt",
    )(proj, proj, proj, proj, cos, sin, decay, qd, kd, gc, gain)


def _ret_sample_body(q_ref, k_ref, v_ref, g_ref, cos_ref, sin_ref, dec_ref, qd_ref, kd_ref,
                     gc_ref, gain_ref, s0_ref, o_ref, s_ref, *, n_req):
    sr = q_ref.shape[1]
    cos = cos_ref[...]
    sin = sin_ref[...]
    q = _rope_half(q_ref[0], cos, sin)
    k = _rope_half(k_ref[0], cos, sin) * (DK_RET ** -0.5)
    v = v_ref[0]
    vb = v.astype(BF16)
    scores = _dot_nt(q.astype(BF16), k.astype(BF16)) * dec_ref[0]
    o = _dot(scores.astype(BF16), vb)
    qdec = q * qd_ref[0]
    kdec = k * kd_ref[0]
    gc = gc_ref[0, 0:1, :1]
    row = lax.broadcasted_iota(jnp.int32, (sr, 1), 0)
    for b in range(n_req):
        mine = jnp.logical_and(row >= b * SROWS, row < (b + 1) * SROWS)
        s0 = s0_ref[b, 0]
        o = o + _dot(jnp.where(mine, qdec, 0.0).astype(BF16), s0.astype(BF16))
        kb = jnp.transpose(jnp.where(mine, kdec, 0.0)).astype(BF16)
        s_ref[b, 0] = gc * s0 + _dot(kb, vb)
    o_ref[...] = _head_ln_gate(o, gain_ref[...], g_ref[0])


def _ret_sample(proj_s, gain, state0, past_len):
    _, sr, width = proj_s.shape
    nh = width // DV_RET
    nreq = sr // SROWS
    t = jnp.arange(sr) % SROWS
    cos, sin, log_g, _ = _ret_tables(past_len + t, DEC_SEQ, nh)
    tf = t.astype(F32)
    diff = tf[:, None] - tf[None, :]
    same = (jnp.arange(sr)[:, None] // SROWS) == (jnp.arange(sr)[None, :] // SROWS)
    real = (t < DEC_SEQ)
    ok = same & (diff >= 0) & real[:, None] & real[None, :]
    decay = jnp.where(ok[None], jnp.exp(jnp.maximum(diff, 0.0)[None] * log_g[:, None, None]), 0.0)
    ones = jnp.ones((1, 1, DK_RET), F32)
    qd = jnp.exp((tf + 1)[None, :, None] * log_g[:, None, None]) * ones
    kd = jnp.where(real[None, :, None],
                   jnp.exp((DEC_SEQ - 1 - tf)[None, :, None] * log_g[:, None, None]), 0.0) * ones
    gc = jnp.exp(DEC_SEQ * log_g)[:, None, None] * jnp.ones((1, SUBLANES, LANES), F32)
    body = functools.partial(_ret_sample_body, n_req=nreq)
    tab = lambda: pl.BlockSpec((sr, DK_RET), lambda h: (0, 0))
    per_head = lambda r, w: pl.BlockSpec((1, r, w), lambda h: (h, 0, 0))
    st = pl.BlockSpec((nreq, 1, DK_RET, DV_RET), lambda h: (0, h, 0, 0))
    return pl.pallas_call(
        body,
        grid=(nh,),
        in_specs=[
            pl.BlockSpec((1, sr, DK_RET), lambda h: (3, 0, h)),
            pl.BlockSpec((1, sr, DK_RET), lambda h: (3, 0, nh + h)),
            pl.BlockSpec((1, sr, DV_RET), lambda h: (4, 0, h)),
            pl.BlockSpec((1, sr, DV_RET), lambda h: (5, 0, h)),
            tab(), tab(),
            per_head(sr, sr), per_head(sr, DK_RET), per_head(sr, DK_RET), per_head(SUBLANES, LANES),
            pl.BlockSpec((1, DV_RET), lambda h: (0, h)),
            st,
        ],
        out_specs=[pl.BlockSpec((sr, DV_RET), lambda h: (0, h)), st],
        out_shape=[jax.ShapeDtypeStruct((sr, width), F32),
                   jax.ShapeDtypeStruct(state0.shape, F32)],
        compiler_params=_cp(("arbitrary",)),
        name="ret_sample",
    )(proj_s, proj_s, proj_s, proj_s, cos, sin, decay, qd, kd, gc, gain, state0)


def _diff_rope_tables(pos):
    half = ROT_DIM // 2
    inv = np.float32(ROPE_THETA) ** (-np.arange(half, dtype=np.float32) / half)
    ang = pos.astype(F32)[:, None] * jnp.asarray(inv)[None, :]
    cos, sin = jnp.cos(ang), jnp.sin(ang)
    n = pos.shape[0]
    rest = HD_DIFF - ROT_DIM
    c = jnp.concatenate([cos, cos, jnp.ones((n, rest), F32)], axis=1)
    s1 = jnp.concatenate([-sin, jnp.zeros((n, half + rest), F32)], axis=1)
    s2 = jnp.concatenate([jnp.zeros((n, half), F32), sin, jnp.zeros((n, rest), F32)], axis=1)
    two = lambda a: jnp.concatenate([a, a], axis=1)
    return two(c), two(s1), two(s2)


def _seg_mean_mat():
    a = np.arange(2 * HD_DIFF)
    return jnp.asarray((a[:, None] // HD_DIFF == a[None, :] // HD_DIFF).astype(np.float32) / HD_DIFF,
                       dtype=BF16)


def _qk_norm_rope(x, gain, c, s1, s2, seg):
    hi, lo = _split_bf16(x * x)
    ms = _dot(hi, seg) + _dot(lo, seg)
    y = x * lax.rsqrt(ms + EPS) * gain
    half = ROT_DIM // 2
    return y * c + pltpu.roll(y, LANES - half, axis=1) * s1 + pltpu.roll(y, half, axis=1) * s2


def _diff_prep_body(q_ref, k_ref, qg_ref, kg_ref, c_ref, s1_ref, s2_ref, seg_ref, qo_ref, ko_ref,
                    *, n_heads, q_scale):
    c, s1, s2, seg = c_ref[...], s1_ref[...], s2_ref[...], seg_ref[...]
    for h in range(n_heads):
        cols = slice(h * LANES, (h + 1) * LANES)
        q = _qk_norm_rope(q_ref[0, :, cols], qg_ref[...], c, s1, s2, seg)
        qo_ref[:, cols] = (q * q_scale).astype(qo_ref.dtype)
        ko_ref[:, cols] = _qk_norm_rope(k_ref[0, :, cols], kg_ref[...], c, s1, s2, seg)


def _diff_prep(proj, qk_gain, pos, rows_per_pos_table, q_dtype):
    _, rows, width = proj.shape
    nh = width // LANES
    tm = min(256, rows)
    c, s1, s2 = _diff_rope_tables(pos)
    nt = rows_per_pos_table // tm
    qg = jnp.tile(qk_gain[0], 2)[None, :]
    kg = jnp.tile(qk_gain[1], 2)[None, :]
    body = functools.partial(_diff_prep_body, n_heads=nh, q_scale=HD_DIFF ** -0.5)
    tab = lambda: pl.BlockSpec((tm, LANES), lambda i: (i % nt, 0))
    vec = lambda: pl.BlockSpec((1, LANES), lambda i: (0, 0))
    return pl.pallas_call(
        body,
        grid=(rows // tm,),
        in_specs=[pl.BlockSpec((1, tm, width), lambda i: (0, i, 0)),
                  pl.BlockSpec((1, tm, width), lambda i: (1, i, 0)),
                  vec(), vec(), tab(), tab(), tab(),
                  pl.BlockSpec((LANES, LANES), lambda i: (0, 0))],
        out_specs=[pl.BlockSpec((tm, width), lambda i: (i, 0)),
                   pl.BlockSpec((tm, width), lambda i: (i, 0))],
        out_shape=[jax.ShapeDtypeStruct((rows, width), q_dtype),
                   jax.ShapeDtypeStruct((rows, width), F32)],
        compiler_params=_cp(("arbitrary",)),
        name="diff_prep",
    )(proj, proj, qg, kg, c, s1, s2, _seg_mean_mat())


def _diff_lambda(lam_ref, lam_init):
    lv = lam_ref[...]
    a = jnp.sum(lv[0:1] * lv[1:2], axis=-1, keepdims=True)
    b = jnp.sum(lv[2:3] * lv[3:4], axis=-1, keepdims=True)
    return jnp.exp(a) - jnp.exp(b) + lam_init


def _stack_components(q):
    lane = lax.broadcasted_iota(jnp.int32, q.shape, 1)
    zero = jnp.zeros_like(q)
    return jnp.concatenate([jnp.where(lane < HD_DIFF, q, zero), jnp.where(lane >= HD_DIFF, q, zero)],
                           axis=0)


def _softmax_step(qz, kblk, vblk, m, l, acc, mask):
    s = _dot_nt(qz, kblk)
    if mask is not None:
        s = jnp.where(mask, s, -jnp.inf)
    m_new = jnp.maximum(m, jnp.max(s, axis=-1, keepdims=True))
    alpha = jnp.exp(m - m_new)
    p = jnp.exp(s - m_new)
    l = alpha * l + jnp.sum(p, axis=-1, keepdims=True)
    acc = alpha * acc + _dot(p.astype(BF16), vblk)
    return m_new, l, acc


def _diff_finish(l, acc, n, lam, sub_g, out_scale):
    o = acc[:n] / l[:n] - lam * (acc[n:] / l[n:])
    y = o * lax.rsqrt(jnp.mean(o * o, axis=-1, keepdims=True) + EPS) * sub_g
    return y * out_scale


def _diff_prompt_body(q_ref, k_ref, v_ref, lam_ref, subg_ref, o_ref, kb_scr, vb_scr,
                      *, nq, lam_init):
    blk = ATT_BLK
    kb_scr[...] = k_ref[...].astype(BF16)
    vb_scr[...] = v_ref[0].astype(BF16)
    lam = _diff_lambda(lam_ref, lam_init)
    row = lax.broadcasted_iota(jnp.int32, (2 * blk, blk), 0) & (blk - 1)
    col = lax.broadcasted_iota(jnp.int32, (2 * blk, blk), 1)
    causal = col <= row

    def q_body(qi, c):
        rows = pl.ds(pl.multiple_of(qi * blk, blk), blk)
        qz = _stack_components(q_ref[rows, :])

        def kv_body(kb, st):
            rk = pl.ds(pl.multiple_of(kb * blk, blk), blk)
            return _softmax_step(qz, kb_scr[rk, :], vb_scr[rk, :], *st, None)

        init = (jnp.full((2 * blk, 1), -jnp.inf, F32), jnp.zeros((2 * blk, 1), F32),
                jnp.zeros((2 * blk, DV_DIFF), F32))
        st = lax.fori_loop(0, qi, kv_body, init)
        _, l, acc = _softmax_step(qz, kb_scr[rows, :], vb_scr[rows, :], *st, causal)
        o_ref[rows, :] = _diff_finish(l, acc, blk, lam, subg_ref[...], 1.0 - lam_init).astype(o_ref.dtype)
        return c

    lax.fori_loop(0, nq, q_body, 0)


def _diff_prompt(qn, kn, proj, diff_lambda, sub_g, lam_init):
    tp, width = qn.shape
    nb = tp // SEQ
    nh = width // DV_DIFF
    body = functools.partial(_diff_prompt_body, nq=SEQ // ATT_BLK, lam_init=lam_init)
    return pl.pallas_call(
        body,
        grid=(nb, nh),
        in_specs=[pl.BlockSpec((SEQ, DV_DIFF), lambda b, h: (b, h)),
                  pl.BlockSpec((SEQ, DV_DIFF), lambda b, h: (b, h)),
                  pl.BlockSpec((1, SEQ, DV_DIFF), lambda b, h: (2, b, h)),
                  pl.BlockSpec(diff_lambda.shape, lambda b, h: (0, 0)),
                  pl.BlockSpec((1, DV_DIFF), lambda b, h: (0, 0))],
        out_specs=pl.BlockSpec((SEQ, DV_DIFF), lambda b, h: (b, h)),
        out_shape=jax.ShapeDtypeStruct((tp, width), BF16),
        scratch_shapes=[pltpu.VMEM((SEQ, DV_DIFF), BF16), pltpu.VMEM((SEQ, DV_DIFF), BF16)],
        compiler_params=_cp(("arbitrary", "arbitrary")),
        name="diff_prompt",
    )(qn, kn, proj, diff_lambda, sub_g[None, :])


def _diff_sample_body(pt_ref, q_ref, kn_ref, vn_ref, kc_ref, vc_ref, lam_ref, subg_ref, o_ref,
                      m_scr, l_scr, acc_scr, *, n_heads, n_pages, lam_init):
    s = pl.program_id(1)
    n2 = 2 * SROWS

    @pl.when(s == 0)
    def _():
        m_scr[...] = jnp.full(m_scr.shape, -jnp.inf, F32)
        l_scr[...] = jnp.zeros_like(l_scr)
        acc_scr[...] = jnp.zeros_like(acc_scr)

    def step(h, kblk, vblk, mask):
        cols = slice(h * LANES, (h + 1) * LANES)
        rows = slice(h * n2, (h + 1) * n2)
        qz = _stack_components(q_ref[:, cols]).astype(BF16)
        m, l, acc = _softmax_step(qz, kblk, vblk, m_scr[rows, :1], l_scr[rows, :1],
                                  acc_scr[rows, :], mask)
        m_scr[rows, :] = jnp.broadcast_to(m, (n2, LANES))
        l_scr[rows, :] = jnp.broadcast_to(l, (n2, LANES))
        acc_scr[rows, :] = acc

    for h in range(n_heads):
        step(h, kc_ref[0, 0, pl.ds(h, PAGE_SIZE, stride=n_heads), :].astype(BF16),
             vc_ref[0, 0, pl.ds(h, PAGE_SIZE, stride=n_heads), :].astype(BF16), None)

    @pl.when(s == n_pages - 1)
    def _():
        lam = _diff_lambda(lam_ref, lam_init)
        row = lax.broadcasted_iota(jnp.int32, (n2, ATT_BLK), 0) & (SROWS - 1)
        col = lax.broadcasted_iota(jnp.int32, (n2, ATT_BLK), 1)
        causal = col <= row
        pad = jnp.zeros((ATT_BLK - SROWS, LANES), F32)
        for h in range(n_heads):
            cols = slice(h * LANES, (h + 1) * LANES)
            rows = slice(h * n2, (h + 1) * n2)
            step(h, jnp.concatenate([kn_ref[:, cols], pad], axis=0).astype(BF16),
                 jnp.concatenate([vn_ref[0, :, cols], pad], axis=0).astype(BF16), causal)
            o_ref[:, cols] = _diff_finish(l_scr[rows, :1], acc_scr[rows, :], SROWS, lam,
                                          subg_ref[...], 1.0 - lam_init)


def _diff_sample(qn_s, kn_s, proj_s, cache_k, cache_v, o, page_table, diff_lambda, sub_g, lam_init):
    sr, width = qn_s.shape
    nh = width // DV_DIFF
    nreq, n_pages = page_table.shape
    body = functools.partial(_diff_sample_body, n_heads=nh, n_pages=n_pages, lam_init=lam_init)
    page = lambda b, s, pt: (o, pt[b, s], 0, 0)
    return pl.pallas_call(
        body,
        grid_spec=pltpu.PrefetchScalarGridSpec(
            num_scalar_prefetch=1,
            grid=(nreq, n_pages),
            in_specs=[pl.BlockSpec((SROWS, width), lambda b, s, pt: (b, 0)),
                      pl.BlockSpec((SROWS, width), lambda b, s, pt: (b, 0)),
                      pl.BlockSpec((1, SROWS, width), lambda b, s, pt: (2, b, 0)),
                      pl.BlockSpec((1, 1, PAGE_SIZE * nh, DV_DIFF), page),
                      pl.BlockSpec((1, 1, PAGE_SIZE * nh, DV_DIFF), page),
                      pl.BlockSpec(diff_lambda.shape, lambda b, s, pt: (0, 0)),
                      pl.BlockSpec((1, DV_DIFF), lambda b, s, pt: (0, 0))],
            out_specs=pl.BlockSpec((SROWS, width), lambda b, s, pt: (b, 0)),
            scratch_shapes=[pltpu.VMEM((nh * 2 * SROWS, LANES), F32),
                            pltpu.VMEM((nh * 2 * SROWS, LANES), F32),
                            pltpu.VMEM((nh * 2 * SROWS, DV_DIFF), F32)],
        ),
        out_shape=jax.ShapeDtypeStruct((sr, width), F32),
        compiler_params=_cp(("arbitrary", "arbitrary")),
        name="diff_sample",
    )(page_table, qn_s, kn_s, proj_s, cache_k, cache_v, diff_lambda, sub_g[None, :])


def _route(h, whi_ref, wlo_ref, b_ref):
    hi, lo = _split_bf16(h)
    logits = _dot(hi, whi_ref[...]) + _dot(lo, whi_ref[...]) + _dot(hi, wlo_ref[...]) + b_ref[...]
    lane = lax.broadcasted_iota(jnp.int32, logits.shape, 1).astype(F32)
    neg = -jnp.inf
    first = lambda hit: jnp.min(jnp.where(hit, lane, float(LANES)), axis=-1, keepdims=True)
    gl = jnp.where(lane < N_GROUPS, logits, neg)
    gmax = jnp.max(gl, axis=-1, keepdims=True)
    g_top = 1.0 / jnp.sum(jnp.exp(gl - gmax), axis=-1, keepdims=True)
    g_idx = first(gl == gmax)
    lo_lane = N_GROUPS + EXP_PER_GROUP * g_idx
    el = jnp.where(jnp.logical_and(lane >= lo_lane, lane < lo_lane + EXP_PER_GROUP), logits, neg)
    emax = jnp.max(el, axis=-1, keepdims=True)
    esum = jnp.sum(jnp.exp(el - emax), axis=-1, keepdims=True)
    l1 = first(el == emax)
    el2 = jnp.where(lane == l1, neg, el)
    e2max = jnp.max(el2, axis=-1, keepdims=True)
    l2 = first(el2 == e2max)
    p1 = 1.0 / esum
    p2 = jnp.exp(e2max - emax) / esum
    w1 = p1 / (p1 + p2) * g_top
    w2 = p2 / (p1 + p2) * g_top
    rec = jnp.where(lane == 0, l1 - N_GROUPS,
                    jnp.where(lane == 1, l2 - N_GROUPS,
                              jnp.where(lane == 2, w1, jnp.where(lane == 3, w2, 0.0))))
    return hi, rec


def _router_body(xp_ref, xs_ref, g_ref, sh_ref, sc_ref, shs_ref, scs_ref, whi_ref, wlo_ref, b_ref,
                 h_ref, rec_ref, *, n_p, tiles_per_batch):
    i = pl.program_id(0)
    tm = xp_ref.shape[0]
    sr = xs_ref.shape[0]

    @pl.when(i < n_p)
    def _():
        b = i // tiles_per_batch
        sh = sh_ref[0, 0, pl.ds(b, 1), :]
        sc = sc_ref[0, 0, pl.ds(b, 1), :]
        chunk = min(256, tm)

        def body(r, c):
            sl = pl.ds(pl.multiple_of(r * chunk, chunk), chunk)
            hi, rec = _route(_modulate(xp_ref[sl, :], g_ref[...], sh, sc), whi_ref, wlo_ref, b_ref)
            h_ref[sl, :] = hi
            rec_ref[sl, :] = rec
            return c

        lax.fori_loop(0, tm // chunk, body, 0)

    @pl.when(i == n_p)
    def _():
        hi, rec = _route(_mod_sample_tile(xs_ref, g_ref[...], shs_ref, scs_ref), whi_ref, wlo_ref, b_ref)
        h_ref[:sr, :] = hi
        rec_ref[:sr, :] = rec
        h_ref[sr:, :] = jnp.zeros((tm - sr, h_ref.shape[1]), h_ref.dtype)
        rec_ref[sr:, :] = jnp.zeros((tm - sr, LANES), F32)


def _router(xp, xs, g, mods, li, mod_s, w_rg, b_rg, w_re, b_re):
    tp, d = xp.shape
    sr = xs.shape[0]
    tm = _prompt_tile()
    n_p = tp // tm
    last = n_p - 1
    wr = jnp.concatenate([w_rg, jnp.moveaxis(w_re, 0, 1).reshape(d, N_EXPERTS),
                          jnp.zeros((d, LANES - N_GROUPS - N_EXPERTS), F32)], axis=1)
    whi = wr.astype(BF16)
    wlo = (wr - whi.astype(F32)).astype(BF16)
    bias = jnp.concatenate([b_rg, b_re.reshape(-1), jnp.zeros((LANES - N_GROUPS - N_EXPERTS,), F32)])[None]
    body = functools.partial(_router_body, n_p=n_p, tiles_per_batch=SEQ // tm)
    full = lambda r, c: pl.BlockSpec((r, c), lambda i: (0, 0))
    rows = tp + tm
    return pl.pallas_call(
        body,
        grid=(n_p + 1,),
        in_specs=[pl.BlockSpec((tm, d), lambda i: (jnp.minimum(i, last), 0)),
                  full(sr, d),
                  pl.BlockSpec((1, d), lambda i: (0, 0)),
                  pl.BlockSpec((1, 1, MOD_ROWS, d), lambda i: (li, 3, 0, 0)),
                  pl.BlockSpec((1, 1, MOD_ROWS, d), lambda i: (li, 4, 0, 0)),
                  full(sr, d), full(sr, d), full(d, LANES), full(d, LANES), full(1, LANES)],
        out_specs=[pl.BlockSpec((tm, d), lambda i: (i, 0)),
                   pl.BlockSpec((tm, LANES), lambda i: (i, 0))],
        out_shape=[jax.ShapeDtypeStruct((rows, d), BF16),
                   jax.ShapeDtypeStruct((rows, LANES), F32)],
        compiler_params=_cp(("arbitrary",)),
        name="router",
    )(xp, xs, g, mods, mods, mod_s[3], mod_s[4], whi, wlo, bias)


def _experts_body(te_ref, nv_ref, x_ref, wi_ref, wo_ref, y_ref, wi_scr, wo_scr):
    t = pl.program_id(0)
    f = wo_ref.shape[2]
    fresh = jnp.logical_or(t == 0, te_ref[t] != te_ref[jnp.maximum(t - 1, 0)])

    @pl.when(jnp.logical_and(t < nv_ref[0], fresh))
    def _():
        wi_scr[...] = wi_ref[0, 0].astype(BF16)
        wo_scr[...] = wo_ref[0, 0].astype(BF16)

    @pl.when(t < nv_ref[0])
    def _():
        hid = _dot(x_ref[...], wi_scr[...])
        gate = hid[:, f:]
        act = (gate * jax.nn.sigmoid(gate)) * hid[:, :f]
        y_ref[...] = _dot(act.astype(BF16), wo_scr[...]).astype(y_ref.dtype)

    @pl.when(t >= nv_ref[0])
    def _():
        y_ref[...] = jnp.zeros_like(y_ref)


def _experts(xs_sorted, tile_expert, n_valid, w_in, w_out, li):
    p, d = xs_sorted.shape
    f2 = w_in.shape[3]
    return pl.pallas_call(
        _experts_body,
        grid_spec=pltpu.PrefetchScalarGridSpec(
            num_scalar_prefetch=2,
            grid=(p // EXP_TILE,),
            in_specs=[pl.BlockSpec((EXP_TILE, d), lambda t, te, nv: (t, 0)),
                      pl.BlockSpec((1, 1, d, f2), lambda t, te, nv: (li, te[t], 0, 0)),
                      pl.BlockSpec((1, 1, f2 // 2, d), lambda t, te, nv: (li, te[t], 0, 0))],
            out_specs=pl.BlockSpec((EXP_TILE, d), lambda t, te, nv: (t, 0)),
            scratch_shapes=[pltpu.VMEM((d, f2), BF16), pltpu.VMEM((f2 // 2, d), BF16)],
        ),
        out_shape=jax.ShapeDtypeStruct((p, d), BF16),
        compiler_params=_cp(("arbitrary",)),
        name="experts",
    )(tile_expert, n_valid, xs_sorted, w_in, w_out)


def _combine_body(y1p_ref, y2p_ref, y1s_ref, y2s_ref, recp_ref, recs_ref, xp_ref, xs_ref,
                  gate_ref, gates_ref, op_ref, os_ref, *, n_p, tiles_per_batch):
    i = pl.program_id(0)

    def mix(y1, y2, rec):
        return rec[:, 2:3] * y1.astype(F32) + rec[:, 3:4] * y2.astype(F32)

    @pl.when(i < n_p)
    def _():
        b = i // tiles_per_batch
        gate = gate_ref[0, 0, pl.ds(b, 1), :]
        op_ref[...] = xp_ref[...] + gate * mix(y1p_ref[0], y2p_ref[0], recp_ref[...])

    @pl.when(i == n_p)
    def _():
        os_ref[...] = xs_ref[...] + gates_ref[...] * mix(y1s_ref[0], y2s_ref[0], recs_ref[...])


def _combine(yg, rec, xp, xs, mods, li, mod_s):
    tp, d = xp.shape
    sr = xs.shape[0]
    tm = min(512, SEQ)
    n_p = tp // tm
    last = n_p - 1
    stile = tp // sr
    body = functools.partial(_combine_body, n_p=n_p, tiles_per_batch=SEQ // tm)
    pmap = lambda i: (jnp.minimum(i, last), 0)
    return pl.pallas_call(
        body,
        grid=(n_p + 1,),
        in_specs=[pl.BlockSpec((1, tm, d), lambda i: (0, jnp.minimum(i, last), 0)),
                  pl.BlockSpec((1, tm, d), lambda i: (1, jnp.minimum(i, last), 0)),
                  pl.BlockSpec((1, sr, d), lambda i: (0, stile, 0)),
                  pl.BlockSpec((1, sr, d), lambda i: (1, stile, 0)),
                  pl.BlockSpec((tm, LANES), pmap),
                  pl.BlockSpec((sr, LANES), lambda i: (stile, 0)),
                  pl.BlockSpec((tm, d), pmap),
                  pl.BlockSpec((sr, d), lambda i: (0, 0)),
                  pl.BlockSpec((1, 1, MOD_ROWS, d), lambda i: (li, 5, 0, 0)),
                  pl.BlockSpec((sr, d), lambda i: (0, 0))],
        out_specs=[pl.BlockSpec((tm, d), pmap), pl.BlockSpec((sr, d), lambda i: (0, 0))],
        out_shape=[jax.ShapeDtypeStruct((tp, d), F32), jax.ShapeDtypeStruct((sr, d), F32)],
        compiler_params=_cp(("arbitrary",)),
        name="combine",
    )(yg, yg, yg, yg, rec, rec, xp, xs, mods, mod_s[5])


def _moe(xp, xs, g, mods, li, mod_s, w_rg, b_rg, w_re, b_re, w_in, w_out):
    tp, d = xp.shape
    sr = xs.shape[0]
    h_all, rec = _router(xp, xs, g, mods, li, mod_s, w_rg, b_rg, w_re, b_re)
    nt = tp + sr
    zero_row = nt
    row = jnp.arange(nt)
    real = jnp.logical_or(row < tp, (row - tp) % SROWS < DEC_SEQ)
    eid = jnp.where(real[:, None], rec[:nt, :2].astype(jnp.int32), N_EXPERTS)
    flat = eid.T.reshape(-1)
    onehot = (flat[:, None] == jnp.arange(N_EXPERTS)[None, :]).astype(jnp.int32)
    before = jnp.cumsum(onehot, axis=0) - onehot
    rank = jnp.sum(before * onehot, axis=1)
    count = jnp.sum(onehot, axis=0)
    padded = (count + EXP_TILE - 1) // EXP_TILE * EXP_TILE
    ends = jnp.cumsum(padded)
    start = ends - padded
    n_real = 2 * (tp + (sr // SROWS) * DEC_SEQ)
    p = (n_real + N_EXPERTS * (EXP_TILE - 1) + EXP_TILE - 1) // EXP_TILE * EXP_TILE
    is_real = flat < N_EXPERTS
    pos = jnp.where(is_real, start[jnp.minimum(flat, N_EXPERTS - 1)] + rank, p)
    src = jnp.full((p,), zero_row, jnp.int32).at[pos].set(jnp.tile(row, 2), mode="drop")
    tile_start = jnp.arange(p // EXP_TILE) * EXP_TILE
    tile_expert = jnp.minimum(jnp.searchsorted(ends, tile_start, side="right"),
                              N_EXPERTS - 1).astype(jnp.int32)
    n_valid = (ends[-1] // EXP_TILE).astype(jnp.int32)[None]
    xs_sorted = jnp.take(h_all, src, axis=0)
    ys = _experts(xs_sorted, tile_expert, n_valid, w_in, w_out, li)
    yg = jnp.take(ys, jnp.minimum(pos, p - 1).reshape(2, nt), axis=0)
    return _combine(yg, rec, xp, xs, mods, li, mod_s)


def kernel(x_prompt, x_sample, cache_sb_k, cache_sb_v, state_ret, cache_diff_k, cache_diff_v, page_table, c_prompt, c_sample, w_ada, b_ada, norm_mix, norm_ffn, w_in_even, w_out_even, ret_gn_gain, w_in_odd, w_out_odd, qk_gain, diff_lambda, diff_subln, w_router_group, b_router_group, w_router_expert, b_router_expert, w_expert_in, w_expert_out):
    nb, seq, d = x_prompt.shape
    nreq, dec_seq, _ = x_sample.shape
    n_pages = page_table.shape[1]
    past_len = n_pages * PAGE_SIZE
    tp = nb * seq
    sr = nreq * SROWS
    h_sb = cache_sb_k.shape[3]
    h_diff = cache_diff_k.shape[3]

    xp = x_prompt.reshape(tp, d)
    xs = jnp.pad(x_sample, ((0, 0), (0, SROWS - dec_seq), (0, 0))).reshape(sr, d)
    c_all = jnp.concatenate([c_prompt, c_sample, jnp.zeros((MOD_ROWS - nb - nreq, d), F32)], axis=0)
    mods = _ada(c_all, w_ada, b_ada)
    uu = _cumsum_mat(ATT_BLK)
    csk = cache_sb_k.reshape(cache_sb_k.shape[0], cache_sb_k.shape[1], PAGE_SIZE * h_sb, HD_SB)
    csv = cache_sb_v.reshape(csk.shape)
    cdk = cache_diff_k.reshape(cache_diff_k.shape[0], cache_diff_k.shape[1], PAGE_SIZE * h_diff, DV_DIFF)
    cdv = cache_diff_v.reshape(cdk.shape)
    pos_s = past_len + (jnp.arange(sr) % SROWS)

    def sample_rows(a, tail):
        return a.reshape(nreq, SROWS, *tail)[:, :dec_seq]

    sbk_p, sbv_p, sbk_s, sbv_s, ret_p, ret_s = [], [], [], [], [], []
    dk_p, dv_p, dk_s, dv_s = [], [], [], []
    for li in range(DEPTH):
        mod_s = [jnp.repeat(mods[li, k, nb:nb + nreq], SROWS, axis=0) for k in range(ADA_MOD)]
        if li % 2 == 0:
            e = li // 2
            w_in = w_in_even[e].astype(BF16)
            proj_p, proj_s = _inproj(xp, xs, norm_mix[li][None], mods, li, 0, 1, mod_s, w_in, h_sb * HD_SB)
            gain = ret_gn_gain[e][None, :]
            o_sb_p = _sb_prompt(proj_p, uu)
            o_r_p, s_p = _ret_prompt(proj_p, gain)
            o_sb_s = _sb_sample(proj_s, csk, csv, e, page_table, uu)
            o_r_s, s_s = _ret_sample(proj_s, gain, state_ret[e], past_len)
            xp, xs = _outproj((o_sb_p, o_r_p), (o_sb_s, o_r_s), w_out_even[e].astype(BF16),
                              xp, xs, mods, li, 2, mod_s)
            sbk_p.append(proj_p[1].reshape(nb, seq, h_sb, HD_SB))
            sbv_p.append(proj_p[2].reshape(nb, seq, h_sb, HD_SB))
            sbk_s.append(sample_rows(proj_s[1], (h_sb, HD_SB)))
            sbv_s.append(sample_rows(proj_s[2], (h_sb, HD_SB)))
            ret_p.append(s_p)
            ret_s.append(s_s)
        else:
            o = li // 2
            lam_init = 0.8 - 0.6 * math.exp(-0.3 * li)
            w_in = w_in_odd[o].astype(BF16)
            proj_p, proj_s = _inproj(xp, xs, norm_mix[li][None], mods, li, 0, 1, mod_s, w_in, h_diff * DV_DIFF)
            qn_p, kn_p = _diff_prep(proj_p, qk_gain[o], jnp.arange(seq), seq, BF16)
            qn_s, kn_s = _diff_prep(proj_s, qk_gain[o], pos_s, sr, F32)
            o_p = _diff_prompt(qn_p, kn_p, proj_p, diff_lambda[o], diff_subln[o], lam_init)
            o_s = _diff_sample(qn_s, kn_s, proj_s, cdk, cdv, o, page_table, diff_lambda[o],
                               diff_subln[o], lam_init)
            xp, xs = _outproj(o_p, o_s, w_out_odd[o].astype(BF16), xp, xs, mods, li, 2, mod_s)
            dk_p.append(kn_p.reshape(nb, seq, h_diff, DV_DIFF))
            dv_p.append(proj_p[2].reshape(nb, seq, h_diff, DV_DIFF))
            dk_s.append(sample_rows(kn_s, (h_diff, DV_DIFF)))
            dv_s.append(sample_rows(proj_s[2], (h_diff, DV_DIFF)))
        xp, xs = _moe(xp, xs, norm_ffn[li][None], mods, li, mod_s, w_router_group[li], b_router_group[li],
                      w_router_expert[li], b_router_expert[li], w_expert_in, w_expert_out)
    return (xp.reshape(nb, seq, d), sample_rows(xs, (d,)),
            jnp.stack(sbk_p), jnp.stack(sbv_p), jnp.stack(sbk_s), jnp.stack(sbv_s),
            jnp.stack(ret_p), jnp.stack(ret_s), jnp.stack(dk_p), jnp.stack(dv_p),
            jnp.stack(dk_s), jnp.stack(dv_s))
```

```python
import functools
import math

import jax
import jax.numpy as jnp
import numpy as np
from jax import lax
from jax.experimental import pallas as pl
from jax.experimental.pallas import tpu as pltpu

D_MODEL = 2048
BATCH = 4
SEQ = 2048
DEPTH = 4
DEC_BATCH = 8
DEC_SEQ = 4
PAGE_SIZE = 128
EPS = 1e-6
HD_SB = 128
DK_RET = 128
DV_RET = 256
RET_THETA = 10000.0
HD_DIFF = 64
DV_DIFF = 2 * HD_DIFF
ROPE_THETA = 500000.0
ROT_DIM = HD_DIFF // 4
N_GROUPS = 4
EXP_PER_GROUP = 4
N_EXPERTS = N_GROUPS * EXP_PER_GROUP
D_EXPERT = 512
ADA_MOD = 6

F32 = jnp.float32
BF16 = jnp.bfloat16

LANES = 128
SUBLANES = 8
MOD_ROWS = 16
SROWS = SUBLANES
ATT_BLK = 128
EXP_TILE = 256
SB_EXIT = -104.0
VMEM_LIMIT = 56 * 2**20


def _cp(sem, vmem=VMEM_LIMIT):
    return pltpu.CompilerParams(dimension_semantics=sem, vmem_limit_bytes=vmem)


def _dot(a, b):
    return jnp.dot(a, b, preferred_element_type=F32)


def _dot_nt(a, b):
    return lax.dot_general(a, b, (((1,), (1,)), ((), ())), preferred_element_type=F32)


def _split_bf16(x):
    hi = x.astype(BF16)
    lo = (x - hi.astype(F32)).astype(BF16)
    return hi, lo


def _modulate(x, g, shift, scale):
    y = x * lax.rsqrt(jnp.mean(x * x, axis=-1, keepdims=True) + EPS) * g
    return y * (1.0 + scale) + shift


def _sample_row_valid(rows):
    r = lax.broadcasted_iota(jnp.int32, (rows, 1), 0)
    return (r & (SROWS - 1)) < DEC_SEQ


def _ada_body(c_ref, w_ref, b_ref, o_ref):
    c = c_ref[...]
    a = (c * jax.nn.sigmoid(c)).astype(BF16)
    o_ref[0, 0] = _dot(a, w_ref[0].astype(BF16)) + b_ref[0]


def _ada(c_all, w_ada, b_ada):
    depth, d, _ = w_ada.shape
    tn = min(1024, d)
    nj = d // tn
    return pl.pallas_call(
        _ada_body,
        grid=(depth, ADA_MOD, nj),
        in_specs=[
            pl.BlockSpec((MOD_ROWS, d), lambda l, k, j: (0, 0)),
            pl.BlockSpec((1, d, tn), lambda l, k, j: (l, 0, k * nj + j)),
            pl.BlockSpec((1, 1, tn), lambda l, k, j: (l, 0, k * nj + j)),
        ],
        out_specs=pl.BlockSpec((1, 1, MOD_ROWS, tn), lambda l, k, j: (l, k, 0, j)),
        out_shape=jax.ShapeDtypeStruct((depth, ADA_MOD, MOD_ROWS, d), F32),
        compiler_params=_cp(("arbitrary",) * 3),
        name="ada",
    )(c_all, w_ada, b_ada.reshape(depth, 1, -1))


def _prompt_tile():
    return min(1024, SEQ)


def _mod_prompt_tile(x_ref, h_ref, g, sh, sc, rows):
    chunk = min(256, rows)

    def body(r, c):
        sl = pl.ds(pl.multiple_of(r * chunk, chunk), chunk)
        h_ref[sl, :] = _modulate(x_ref[sl, :], g, sh, sc).astype(h_ref.dtype)
        return c

    lax.fori_loop(0, rows // chunk, body, 0)


def _mod_sample_tile(xs_ref, g, shs_ref, scs_ref):
    h = _modulate(xs_ref[...], g, shs_ref[...], scs_ref[...])
    return jnp.where(_sample_row_valid(xs_ref.shape[0]), h, 0.0)


def _inproj_body(xp_ref, xs_ref, g_ref, sh_ref, sc_ref, shs_ref, scs_ref, w_ref,
                 op_ref, os_ref, hp_scr, hs_scr, *, n_p, tiles_per_batch):
    i = pl.program_id(0)
    j = pl.program_id(1)
    tm = xp_ref.shape[0]

    @pl.when(jnp.logical_and(i < n_p, j == 0))
    def _():
        b = i // tiles_per_batch
        _mod_prompt_tile(xp_ref, hp_scr, g_ref[...], sh_ref[0, 0, pl.ds(b, 1), :],
                         sc_ref[0, 0, pl.ds(b, 1), :], tm)

    @pl.when(jnp.logical_and(i == n_p, j == 0))
    def _():
        hs_scr[...] = _mod_sample_tile(xs_ref, g_ref[...], shs_ref, scs_ref).astype(BF16)

    @pl.when(i < n_p)
    def _():
        chunk = min(512, tm)

        def body(r, c):
            sl = pl.ds(pl.multiple_of(r * chunk, chunk), chunk)
            op_ref[0, sl, :] = _dot(hp_scr[sl, :], w_ref[...])
            return c

        lax.fori_loop(0, tm // chunk, body, 0)

    @pl.when(i == n_p)
    def _():
        os_ref[0] = _dot(hs_scr[...], w_ref[...])


def _inproj(xp, xs, g, mods, li, k_shift, k_scale, mod_s, w, group_w):
    tp, d = xp.shape
    sr = xs.shape[0]
    n = w.shape[1]
    tm = _prompt_tile()
    n_p = tp // tm
    tn = min(1024, group_w)
    nj = n // tn
    tpg = group_w // tn
    last = n_p - 1

    def p_out(i, j):
        jj = jnp.where(i == n_p, nj - 1, j)
        return (jj // tpg, jnp.minimum(i, last), jj % tpg)

    def s_out(i, j):
        jj = jnp.where(i == n_p, j, 0)
        return (jj // tpg, 0, jj % tpg)

    body = functools.partial(_inproj_body, n_p=n_p, tiles_per_batch=SEQ // tm)
    return pl.pallas_call(
        body,
        grid=(n_p + 1, nj),
        in_specs=[
            pl.BlockSpec((tm, d), lambda i, j: (jnp.minimum(i, last), 0)),
            pl.BlockSpec((sr, d), lambda i, j: (0, 0)),
            pl.BlockSpec((1, d), lambda i, j: (0, 0)),
            pl.BlockSpec((1, 1, MOD_ROWS, d), lambda i, j: (li, k_shift, 0, 0)),
            pl.BlockSpec((1, 1, MOD_ROWS, d), lambda i, j: (li, k_scale, 0, 0)),
            pl.BlockSpec((sr, d), lambda i, j: (0, 0)),
            pl.BlockSpec((sr, d), lambda i, j: (0, 0)),
            pl.BlockSpec((d, tn), lambda i, j: (0, j)),
        ],
        out_specs=[
            pl.BlockSpec((1, tm, tn), p_out),
            pl.BlockSpec((1, sr, tn), s_out),
        ],
        out_shape=[
            jax.ShapeDtypeStruct((n // group_w, tp, group_w), F32),
            jax.ShapeDtypeStruct((n // group_w, sr, group_w), F32),
        ],
        scratch_shapes=[pltpu.VMEM((tm, d), BF16), pltpu.VMEM((sr, d), BF16)],
        compiler_params=_cp(("arbitrary", "arbitrary")),
        name="inproj",
    )(xp, xs, g, mods, mods, mod_s[k_shift], mod_s[k_scale], w)


def _outproj_body(a1p_ref, a2p_ref, a1s_ref, a2s_ref, w1_ref, w2_ref, xp_ref, xs_ref,
                  gate_ref, gates_ref, op_ref, os_ref, *, n_p, tiles_per_batch):
    i = pl.program_id(0)
    tm = xp_ref.shape[0]

    @pl.when(i < n_p)
    def _():
        b = i // tiles_per_batch
        gate = gate_ref[0, 0, pl.ds(b, 1), :]
        chunk = min(512, tm)

        def body(r, c):
            sl = pl.ds(pl.multiple_of(r * chunk, chunk), chunk)
            y = _dot(a1p_ref[sl, :], w1_ref[...]) + _dot(a2p_ref[sl, :], w2_ref[...])
            op_ref[sl, :] = xp_ref[sl, :] + gate * y
            return c

        lax.fori_loop(0, tm // chunk, body, 0)

    @pl.when(i == n_p)
    def _():
        y = (_dot(a1s_ref[...].astype(BF16), w1_ref[...])
             + _dot(a2s_ref[...].astype(BF16), w2_ref[...]))
        os_ref[...] = xs_ref[...] + gates_ref[...] * y


def _outproj(ap, as_, w, xp, xs, mods, li, k_gate, mod_s):
    tp, d = xp.shape
    sr = xs.shape[0]
    tm = _prompt_tile()
    n_p = tp // tm
    tn = min(1024, d)
    nj = d // tn
    last = n_p - 1
    half = w.shape[0] // 2
    if isinstance(ap, tuple):
        a1p, a2p, a1s, a2s = ap[0], ap[1], as_[0], as_[1]
        c2 = 0
    else:
        a1p = a2p = ap
        a1s = a2s = as_
        c2 = 1

    def pin(c):
        return lambda i, j: (jnp.minimum(i, last), c)

    def p_out(i, j):
        return (jnp.minimum(i, last), jnp.where(i == n_p, nj - 1, j))

    def s_out(i, j):
        return (0, jnp.where(i == n_p, j, 0))

    body = functools.partial(_outproj_body, n_p=n_p, tiles_per_batch=SEQ // tm)
    return pl.pallas_call(
        body,
        grid=(n_p + 1, nj),
        in_specs=[
            pl.BlockSpec((tm, half), pin(0)),
            pl.BlockSpec((tm, half), pin(c2)),
            pl.BlockSpec((sr, half), lambda i, j: (0, 0)),
            pl.BlockSpec((sr, half), lambda i, j: (0, c2)),
            pl.BlockSpec((half, tn), lambda i, j: (0, j)),
            pl.BlockSpec((half, tn), lambda i, j: (1, j)),
            pl.BlockSpec((tm, tn), lambda i, j: (jnp.minimum(i, last), j)),
            pl.BlockSpec((sr, tn), lambda i, j: (0, j)),
            pl.BlockSpec((1, 1, MOD_ROWS, tn), lambda i, j: (li, k_gate, 0, j)),
            pl.BlockSpec((sr, tn), lambda i, j: (0, j)),
        ],
        out_specs=[pl.BlockSpec((tm, tn), p_out), pl.BlockSpec((sr, tn), s_out)],
        out_shape=[jax.ShapeDtypeStruct((tp, d), F32), jax.ShapeDtypeStruct((sr, d), F32)],
        compiler_params=_cp(("arbitrary", "arbitrary")),
        name="outproj",
    )(a1p, a2p, a1s, a2s, w, w, xp, xs, mods, mod_s[k_gate])


def _cumsum_mat(n):
    j = np.arange(n)[:, None]
    s = np.arange(n)[None, :]
    blk = np.concatenate([(j >= s).astype(np.float32), np.ones((n, n), np.float32)], axis=1)
    return jnp.asarray(np.concatenate([blk, blk], axis=0), dtype=BF16)


def _sb_block(qb, kblk, vblk, uu, carry, mask, scale):
    n = kblk.shape[0]
    z = _dot_nt(qb, kblk) * scale
    lk = -(jnp.maximum(z, 0.0) + jnp.log1p(jnp.exp(-jnp.abs(z))))
    if mask is not None:
        lk = jnp.where(mask, lk, 0.0)
    hi, lo = _split_bf16(lk)
    incl = [None] * (n // ATT_BLK)
    for g in reversed(range(n // ATT_BLK)):
        cols = slice(g * ATT_BLK, (g + 1) * ATT_BLK)
        r = _dot(jnp.concatenate([hi[:, cols], lo[:, cols]], axis=1), uu)
        incl[g] = r[:, :ATT_BLK] + carry
        carry = carry + r[:, ATT_BLK:]
    a = jnp.exp(z + jnp.concatenate(incl, axis=1))
    if mask is not None:
        a = jnp.where(mask, a, 0.0)
    return carry, _dot(a.astype(BF16), vblk)


def _sb_prompt_body(q_ref, k_ref, v_ref, uu_ref, o_ref, kb_scr, vb_scr, carry_scr, acc_scr,
                    *, nq, tile, n_heads, scale):
    kb_scr[...] = k_ref[0].astype(BF16)
    vb_scr[...] = v_ref[0].astype(BF16)
    row = lax.broadcasted_iota(jnp.int32, (tile, tile), 0)
    col = lax.broadcasted_iota(jnp.int32, (tile, tile), 1)
    strict = col < row
    uu = uu_ref[...]
    head_cols = [slice(h * HD_SB, (h + 1) * HD_SB) for h in range(n_heads)]

    def q_body(qi, c):
        rows = pl.ds(pl.multiple_of(qi * tile, tile), tile)

        def blocks(rk, first):
            mx = None
            for h, cols in enumerate(head_cols):
                qb = q_ref[0, rows, cols].astype(BF16)
                carry0 = jnp.zeros((tile, ATT_BLK), F32) if first else carry_scr[h]
                carry, o = _sb_block(qb, kb_scr[rk, cols], vb_scr[rk, cols], uu, carry0,
                                     strict if first else None, scale)
                carry_scr[h] = carry
                acc_scr[h] = o if first else acc_scr[h] + o
                top = jnp.max(carry)
                mx = top if mx is None else jnp.maximum(mx, top)
            return mx

        def cond(st):
            kb, mx = st
            return jnp.logical_and(kb >= 0, mx > SB_EXIT)

        def body(st):
            kb, _ = st
            return kb - 1, blocks(pl.ds(pl.multiple_of(kb * tile, tile), tile), False)

        lax.while_loop(cond, body, (qi - 1, blocks(rows, True)))
        for h, cols in enumerate(head_cols):
            o_ref[rows, cols] = acc_scr[h].astype(o_ref.dtype)
        return c

    lax.fori_loop(0, nq, q_body, 0)


def _sb_prompt(proj, uu):
    _, tp, width = proj.shape
    nb = tp // SEQ
    nh = width // HD_SB
    hpb = min(2, nh)
    tile = min(256, SEQ)
    w = hpb * HD_SB
    body = functools.partial(_sb_prompt_body, nq=SEQ // tile, tile=tile, n_heads=hpb,
                             scale=HD_SB ** -0.5)
    spec = lambda g: pl.BlockSpec((1, SEQ, w), lambda b, h: (g, b, h))
    return pl.pallas_call(
        body,
        grid=(nb, nh // hpb),
        in_specs=[spec(0), spec(1), spec(2),
                  pl.BlockSpec((2 * ATT_BLK, 2 * ATT_BLK), lambda b, h: (0, 0))],
        out_specs=pl.BlockSpec((SEQ, w), lambda b, h: (b, h)),
        out_shape=jax.ShapeDtypeStruct((tp, width), BF16),
        scratch_shapes=[pltpu.VMEM((SEQ, w), BF16), pltpu.VMEM((SEQ, w), BF16),
                        pltpu.VMEM((hpb, tile, ATT_BLK), F32), pltpu.VMEM((hpb, tile, HD_SB), F32)],
        compiler_params=_cp(("arbitrary", "arbitrary")),
        name="sb_prompt",
    )(proj, proj, proj, uu)


def _sb_sample_body(pt_ref, q_ref, kn_ref, vn_ref, kc_hbm, vc_hbm, uu_ref, o_ref,
                    kbuf, vbuf, sem, carry_scr, acc_scr, *, layer, n_heads, n_pages, scale):
    b = pl.program_id(0)
    blk = ATT_BLK
    uu = uu_ref[...]
    row = lax.broadcasted_iota(jnp.int32, (SROWS, blk), 0)
    col = lax.broadcasted_iota(jnp.int32, (SROWS, blk), 1)
    real_row = row < DEC_SEQ
    strict = col < row
    pad = jnp.zeros((blk - SROWS, HD_SB), F32)

    def blocks(first):
        mx = None
        for h in range(n_heads):
            cols = slice(h * HD_SB, (h + 1) * HD_SB)
            rows = slice(h * SROWS, (h + 1) * SROWS)
            qb = q_ref[0, :, cols].astype(BF16)
            if first:
                kblk = jnp.concatenate([kn_ref[0, :, cols], pad], axis=0).astype(BF16)
                vblk = jnp.concatenate([vn_ref[0, :, cols], pad], axis=0).astype(BF16)
                carry0 = jnp.zeros((SROWS, blk), F32)
            else:
                kblk = kbuf[pl.ds(h, PAGE_SIZE, stride=n_heads), :].astype(BF16)
                vblk = vbuf[pl.ds(h, PAGE_SIZE, stride=n_heads), :].astype(BF16)
                carry0 = carry_scr[rows, :]
            carry, o = _sb_block(qb, kblk, vblk, uu, carry0, strict if first else None, scale)
            carry_scr[rows, :] = carry
            acc_scr[rows, :] = o if first else acc_scr[rows, :] + o
            top = jnp.max(jnp.where(real_row, carry, -jnp.inf))
            mx = top if mx is None else jnp.maximum(mx, top)
        return mx

    def cond(st):
        p, mx = st
        return jnp.logical_and(p >= 0, mx > SB_EXIT)

    def body(st):
        p, _ = st
        page = pt_ref[b, p]
        ck = pltpu.make_async_copy(kc_hbm.at[layer, page], kbuf, sem.at[0])
        cv = pltpu.make_async_copy(vc_hbm.at[layer, page], vbuf, sem.at[1])
        ck.start()
        cv.start()
        ck.wait()
        cv.wait()
        return p - 1, blocks(False)

    lax.while_loop(cond, body, (n_pages - 1, blocks(True)))
    for h in range(n_heads):
        o_ref[:, h * HD_SB:(h + 1) * HD_SB] = acc_scr[h * SROWS:(h + 1) * SROWS, :]


def _sb_sample(proj_s, cache_k, cache_v, e, page_table, uu):
    _, sr, width = proj_s.shape
    nh = width // HD_SB
    nreq, n_pages = page_table.shape
    body = functools.partial(_sb_sample_body, layer=e, n_heads=nh, n_pages=n_pages,
                             scale=HD_SB ** -0.5)
    spec = lambda g: pl.BlockSpec((1, SROWS, width), lambda b, pt: (g, b, 0))
    return pl.pallas_call(
        body,
        grid_spec=pltpu.PrefetchScalarGridSpec(
            num_scalar_prefetch=1,
            grid=(nreq,),
            in_specs=[spec(0), spec(1), spec(2),
                      pl.BlockSpec(memory_space=pl.ANY),
                      pl.BlockSpec(memory_space=pl.ANY),
                      pl.BlockSpec((2 * ATT_BLK, 2 * ATT_BLK), lambda b, pt: (0, 0))],
            out_specs=pl.BlockSpec((SROWS, width), lambda b, pt: (b, 0)),
            scratch_shapes=[pltpu.VMEM((PAGE_SIZE * nh, HD_SB), F32),
                            pltpu.VMEM((PAGE_SIZE * nh, HD_SB), F32),
                            pltpu.SemaphoreType.DMA((2,)),
                            pltpu.VMEM((nh * SROWS, ATT_BLK), F32),
                            pltpu.VMEM((nh * SROWS, HD_SB), F32)],
        ),
        out_shape=jax.ShapeDtypeStruct((sr, width), F32),
        compiler_params=_cp(("arbitrary",)),
        name="sb_sample",
    )(page_table, proj_s, proj_s, proj_s, cache_k, cache_v, uu)


def _ret_tables(pos, chunk, n_heads):
    half = DK_RET // 2
    inv = np.float32(RET_THETA) ** (-np.arange(half, dtype=np.float32) / half)
    ang = pos.astype(F32)[:, None] * jnp.asarray(inv)[None, :]
    cos = jnp.concatenate([jnp.cos(ang), jnp.cos(ang)], axis=1)
    sin = jnp.concatenate([-jnp.sin(ang), jnp.sin(ang)], axis=1)
    log_g = jnp.log1p(-(2.0 ** (-5.0 - jnp.arange(n_heads, dtype=F32))))
    i = jnp.arange(chunk, dtype=F32)
    return cos, sin, log_g, i


def _rope_half(x, cos, sin):
    return x * cos + pltpu.roll(x, x.shape[-1] // 2, axis=1) * sin


def _head_ln_gate(o, gain, g):
    mu = jnp.mean(o, axis=-1, keepdims=True)
    d = o - mu
    var = jnp.mean(d * d, axis=-1, keepdims=True)
    return d * lax.rsqrt(var + EPS) * gain * (g * jax.nn.sigmoid(g))


def _ret_prompt_body(q_ref, k_ref, v_ref, g_ref, cos_ref, sin_ref, dec_ref, qd_ref, kd_ref,
                     gc_ref, gain_ref, o_ref, s_ref, s_scr, *, n_chunks):
    c = ATT_BLK
    s_scr[...] = jnp.zeros_like(s_scr)
    decay = dec_ref[0]
    qd = qd_ref[0]
    kd = kd_ref[0]
    gc = gc_ref[0, 0:1, :1]
    gain = gain_ref[...]

    def body(n, carry):
        rows = pl.ds(pl.multiple_of(n * c, c), c)
        cos = cos_ref[rows, :]
        sin = sin_ref[rows, :]
        q = _rope_half(q_ref[0, rows, :], cos, sin)
        k = _rope_half(k_ref[0, rows, :], cos, sin) * (DK_RET ** -0.5)
        vb = v_ref[0, rows, :].astype(BF16)
        s_prev = s_scr[...]
        scores = _dot_nt(q.astype(BF16), k.astype(BF16)) * decay
        o = _dot(scores.astype(BF16), vb) + _dot((q * qd).astype(BF16), s_prev.astype(BF16))
        kv = _dot(jnp.transpose(k * kd).astype(BF16), vb)
        s_scr[...] = gc * s_prev + kv
        o_ref[rows, :] = _head_ln_gate(o, gain, g_ref[0, rows, :]).astype(o_ref.dtype)
        return carry

    lax.fori_loop(0, n_chunks, body, 0)
    s_ref[0, 0] = s_scr[...]


def _ret_prompt(proj, gain):
    _, tp, width = proj.shape
    nb = tp // SEQ
    nh = width // DV_RET
    c = ATT_BLK
    cos, sin, log_g, i = _ret_tables(jnp.arange(SEQ), c, nh)
    diff = i[:, None] - i[None, :]
    decay = jnp.where(diff >= 0, jnp.exp(jnp.maximum(diff, 0.0)[None] * log_g[:, None, None]), 0.0)
    ones = jnp.ones((1, 1, DK_RET), F32)
    qd = jnp.exp((i + 1)[None, :, None] * log_g[:, None, None]) * ones
    kd = jnp.exp((c - 1 - i)[None, :, None] * log_g[:, None, None]) * ones
    gc = jnp.exp(c * log_g)[:, None, None] * jnp.ones((1, SUBLANES, LANES), F32)
    body = functools.partial(_ret_prompt_body, n_chunks=SEQ // c)
    tab = lambda: pl.BlockSpec((SEQ, DK_RET), lambda b, h: (0, 0))
    per_head = lambda r, w: pl.BlockSpec((1, r, w), lambda b, h: (h, 0, 0))
    return pl.pallas_call(
        body,
        grid=(nb, nh),
        in_specs=[
            pl.BlockSpec((1, SEQ, DK_RET), lambda b, h: (3, b, h)),
            pl.BlockSpec((1, SEQ, DK_RET), lambda b, h: (3, b, nh + h)),
            pl.BlockSpec((1, SEQ, DV_RET), lambda b, h: (4, b, h)),
            pl.BlockSpec((1, SEQ, DV_RET), lambda b, h: (5, b, h)),
            tab(), tab(),
            per_head(c, c), per_head(c, DK_RET), per_head(c, DK_RET), per_head(SUBLANES, LANES),
            pl.BlockSpec((1, DV_RET), lambda b, h: (0, h)),
        ],
        out_specs=[pl.BlockSpec((SEQ, DV_RET), lambda b, h: (b, h)),
                   pl.BlockSpec((1, 1, DK_RET, DV_RET), lambda b, h: (b, h, 0, 0))],
        out_shape=[jax.ShapeDtypeStruct((tp, width), BF16),
                   jax.ShapeDtypeStruct((nb, nh, DK_RET, DV_RET), F32)],
        scratch_shapes=[pltpu.VMEM((DK_RET, DV_RET), F32)],
        compiler_params=_cp(("arbitrary", "arbitrary")),
        name="ret_prompt",
    )(proj, proj, proj, proj, cos, sin, decay, qd, kd, gc, gain)


def _ret_sample_body(q_ref, k_ref, v_ref, g_ref, cos_ref, sin_ref, dec_ref, qd_ref, kd_ref,
                     gc_ref, gain_ref, s0_ref, o_ref, s_ref, *, n_req):
    sr = q_ref.shape[1]
    cos = cos_ref[...]
    sin = sin_ref[...]
    q = _rope_half(q_ref[0], cos, sin)
    k = _rope_half(k_ref[0], cos, sin) * (DK_RET ** -0.5)
    v = v_ref[0]
    vb = v.astype(BF16)
    scores = _dot_nt(q.astype(BF16), k.astype(BF16)) * dec_ref[0]
    o = _dot(scores.astype(BF16), vb)
    qdec = q * qd_ref[0]
    kdec = k * kd_ref[0]
    gc = gc_ref[0, 0:1, :1]
    row = lax.broadcasted_iota(jnp.int32, (sr, 1), 0)
    for b in range(n_req):
        mine = jnp.logical_and(row >= b * SROWS, row < (b + 1) * SROWS)
        s0 = s0_ref[b, 0]
        o = o + _dot(jnp.where(mine, qdec, 0.0).astype(BF16), s0.astype(BF16))
        kb = jnp.transpose(jnp.where(mine, kdec, 0.0)).astype(BF16)
        s_ref[b, 0] = gc * s0 + _dot(kb, vb)
    o_ref[...] = _head_ln_gate(o, gain_ref[...], g_ref[0])


def _ret_sample(proj_s, gain, state0, past_len):
    _, sr, width = proj_s.shape
    nh = width // DV_RET
    nreq = sr // SROWS
    t = jnp.arange(sr) % SROWS
    cos, sin, log_g, _ = _ret_tables(past_len + t, DEC_SEQ, nh)
    tf = t.astype(F32)
    diff = tf[:, None] - tf[None, :]
    same = (jnp.arange(sr)[:, None] // SROWS) == (jnp.arange(sr)[None, :] // SROWS)
    real = (t < DEC_SEQ)
    ok = same & (diff >= 0) & real[:, None] & real[None, :]
    decay = jnp.where(ok[None], jnp.exp(jnp.maximum(diff, 0.0)[None] * log_g[:, None, None]), 0.0)
    ones = jnp.ones((1, 1, DK_RET), F32)
    qd = jnp.exp((tf + 1)[None, :, None] * log_g[:, None, None]) * ones
    kd = jnp.where(real[None, :, None],
                   jnp.exp((DEC_SEQ - 1 - tf)[None, :, None] * log_g[:, None, None]), 0.0) * ones
    gc = jnp.exp(DEC_SEQ * log_g)[:, None, None] * jnp.ones((1, SUBLANES, LANES), F32)
    body = functools.partial(_ret_sample_body, n_req=nreq)
    tab = lambda: pl.BlockSpec((sr, DK_RET), lambda h: (0, 0))
    per_head = lambda r, w: pl.BlockSpec((1, r, w), lambda h: (h, 0, 0))
    st = pl.BlockSpec((nreq, 1, DK_RET, DV_RET), lambda h: (0, h, 0, 0))
    return pl.pallas_call(
        body,
        grid=(nh,),
        in_specs=[
            pl.BlockSpec((1, sr, DK_RET), lambda h: (3, 0, h)),
            pl.BlockSpec((1, sr, DK_RET), lambda h: (3, 0, nh + h)),
            pl.BlockSpec((1, sr, DV_RET), lambda h: (4, 0, h)),
            pl.BlockSpec((1, sr, DV_RET), lambda h: (5, 0, h)),
            tab(), tab(),
            per_head(sr, sr), per_head(sr, DK_RET), per_head(sr, DK_RET), per_head(SUBLANES, LANES),
            pl.BlockSpec((1, DV_RET), lambda h: (0, h)),
            st,
        ],
        out_specs=[pl.BlockSpec((sr, DV_RET), lambda h: (0, h)), st],
        out_shape=[jax.ShapeDtypeStruct((sr, width), F32),
                   jax.ShapeDtypeStruct(state0.shape, F32)],
        compiler_params=_cp(("arbitrary",)),
        name="ret_sample",
    )(proj_s, proj_s, proj_s, proj_s, cos, sin, decay, qd, kd, gc, gain, state0)


def _diff_rope_tables(pos):
    half = ROT_DIM // 2
    inv = np.float32(ROPE_THETA) ** (-np.arange(half, dtype=np.float32) / half)
    ang = pos.astype(F32)[:, None] * jnp.asarray(inv)[None, :]
    cos, sin = jnp.cos(ang), jnp.sin(ang)
    n = pos.shape[0]
    rest = HD_DIFF - ROT_DIM
    c = jnp.concatenate([cos, cos, jnp.ones((n, rest), F32)], axis=1)
    s1 = jnp.concatenate([-sin, jnp.zeros((n, half + rest), F32)], axis=1)
    s2 = jnp.concatenate([jnp.zeros((n, half), F32), sin, jnp.zeros((n, rest), F32)], axis=1)
    two = lambda a: jnp.concatenate([a, a], axis=1)
    return two(c), two(s1), two(s2)


def _seg_mean_mat():
    a = np.arange(2 * HD_DIFF)
    return jnp.asarray((a[:, None] // HD_DIFF == a[None, :] // HD_DIFF).astype(np.float32) / HD_DIFF,
                       dtype=BF16)


def _qk_norm_rope(x, gain, c, s1, s2, seg):
    hi, lo = _split_bf16(x * x)
    ms = _dot(hi, seg) + _dot(lo, seg)
    y = x * lax.rsqrt(ms + EPS) * gain
    half = ROT_DIM // 2
    return y * c + pltpu.roll(y, LANES - half, axis=1) * s1 + pltpu.roll(y, half, axis=1) * s2


def _diff_prep_body(q_ref, k_ref, qg_ref, kg_ref, c_ref, s1_ref, s2_ref, seg_ref, qo_ref, ko_ref,
                    *, n_heads, q_scale):
    c, s1, s2, seg = c_ref[...], s1_ref[...], s2_ref[...], seg_ref[...]
    for h in range(n_heads):
        cols = slice(h * LANES, (h + 1) * LANES)
        q = _qk_norm_rope(q_ref[0, :, cols], qg_ref[...], c, s1, s2, seg)
        qo_ref[:, cols] = (q * q_scale).astype(qo_ref.dtype)
        ko_ref[:, cols] = _qk_norm_rope(k_ref[0, :, cols], kg_ref[...], c, s1, s2, seg)


def _diff_prep(proj, qk_gain, pos, rows_per_pos_table, q_dtype):
    _, rows, width = proj.shape
    nh = width // LANES
    tm = min(256, rows)
    c, s1, s2 = _diff_rope_tables(pos)
    nt = rows_per_pos_table // tm
    qg = jnp.tile(qk_gain[0], 2)[None, :]
    kg = jnp.tile(qk_gain[1], 2)[None, :]
    body = functools.partial(_diff_prep_body, n_heads=nh, q_scale=HD_DIFF ** -0.5)
    tab = lambda: pl.BlockSpec((tm, LANES), lambda i: (i % nt, 0))
    vec = lambda: pl.BlockSpec((1, LANES), lambda i: (0, 0))
    return pl.pallas_call(
        body,
        grid=(rows // tm,),
        in_specs=[pl.BlockSpec((1, tm, width), lambda i: (0, i, 0)),
                  pl.BlockSpec((1, tm, width), lambda i: (1, i, 0)),
                  vec(), vec(), tab(), tab(), tab(),
                  pl.BlockSpec((LANES, LANES), lambda i: (0, 0))],
        out_specs=[pl.BlockSpec((tm, width), lambda i: (i, 0)),
                   pl.BlockSpec((tm, width), lambda i: (i, 0))],
        out_shape=[jax.ShapeDtypeStruct((rows, width), q_dtype),
                   jax.ShapeDtypeStruct((rows, width), F32)],
        compiler_params=_cp(("arbitrary",)),
        name="diff_prep",
    )(proj, proj, qg, kg, c, s1, s2, _seg_mean_mat())


def _diff_lambda(lam_ref, lam_init):
    lv = lam_ref[...]
    a = jnp.sum(lv[0:1] * lv[1:2], axis=-1, keepdims=True)
    b = jnp.sum(lv[2:3] * lv[3:4], axis=-1, keepdims=True)
    return jnp.exp(a) - jnp.exp(b) + lam_init


def _stack_components(q):
    lane = lax.broadcasted_iota(jnp.int32, q.shape, 1)
    zero = jnp.zeros_like(q)
    return jnp.concatenate([jnp.where(lane < HD_DIFF, q, zero), jnp.where(lane >= HD_DIFF, q, zero)],
                           axis=0)


def _softmax_step(qz, kblk, vblk, m, l, acc, mask):
    s = _dot_nt(qz, kblk)
    if mask is not None:
        s = jnp.where(mask, s, -jnp.inf)
    m_new = jnp.maximum(m, jnp.max(s, axis=-1, keepdims=True))
    alpha = jnp.exp(m - m_new)
    p = jnp.exp(s - m_new)
    l = alpha * l + jnp.sum(p, axis=-1, keepdims=True)
    acc = alpha * acc + _dot(p.astype(BF16), vblk)
    return m_new, l, acc


def _diff_finish(l, acc, n, lam, sub_g, out_scale):
    o = acc[:n] / l[:n] - lam * (acc[n:] / l[n:])
    y = o * lax.rsqrt(jnp.mean(o * o, axis=-1, keepdims=True) + EPS) * sub_g
    return y * out_scale


def _diff_prompt_body(q_ref, k_ref, v_ref, lam_ref, subg_ref, o_ref, kb_scr, vt_scr, qt_scr, qz_scr,
                      m_scr, l_scr, acc_scr, *, nq, tile, n_heads, lam_init):
    kb_scr[...] = k_ref[...].astype(BF16)
    head_cols = [slice(h * DV_DIFF, (h + 1) * DV_DIFF) for h in range(n_heads)]
    for h, cols in enumerate(head_cols):
        for j in range(nq):
            blk = slice(j * tile, (j + 1) * tile)
            vt_scr[h, j] = jnp.transpose(v_ref[0, blk, cols]).astype(BF16)
            qt_scr[h, j] = jnp.transpose(q_ref[blk, cols].astype(F32)).astype(BF16)
    lam = _diff_lambda(lam_ref, lam_init)
    key = lax.broadcasted_iota(jnp.int32, (tile, 2 * tile), 0)
    qry = lax.broadcasted_iota(jnp.int32, (tile, 2 * tile), 1) & (tile - 1)
    causal = key <= qry
    first_comp = lax.broadcasted_iota(jnp.int32, (DV_DIFF, tile), 0) < HD_DIFF

    def steps(kb, rk, mask):
        for h, cols in enumerate(head_cols):
            s = _dot(kb_scr[rk, cols], qz_scr[h])
            if mask is not None:
                s = jnp.where(mask, s, -jnp.inf)
            m_old = m_scr[h]
            m_new = jnp.maximum(m_old, jnp.max(s, axis=0, keepdims=True))
            alpha = jnp.exp(m_old - m_new)
            p = jnp.exp(s - m_new)
            l_scr[h] = alpha * l_scr[h] + jnp.sum(p, axis=0, keepdims=True)
            acc_scr[h] = alpha * acc_scr[h] + _dot(vt_scr[h, kb], p.astype(BF16))
            m_scr[h] = m_new

    def q_body(qi, c):
        rows = pl.ds(pl.multiple_of(qi * tile, tile), tile)
        for h in range(n_heads):
            qt = qt_scr[h, qi]
            zero = jnp.zeros_like(qt)
            qz_scr[h] = jnp.concatenate([jnp.where(first_comp, qt, zero),
                                         jnp.where(first_comp, zero, qt)], axis=1)
        m_scr[...] = jnp.full(m_scr.shape, -jnp.inf, F32)
        l_scr[...] = jnp.zeros_like(l_scr)
        acc_scr[...] = jnp.zeros_like(acc_scr)

        def kv_body(kb, c2):
            steps(kb, pl.ds(pl.multiple_of(kb * tile, tile), tile), None)
            return c2

        lax.fori_loop(0, qi, kv_body, 0)
        steps(qi, rows, causal)
        for h, cols in enumerate(head_cols):
            ot = acc_scr[h] / l_scr[h]
            d = ot[:, :tile] - lam * ot[:, tile:]
            y = d * lax.rsqrt(jnp.mean(d * d, axis=0, keepdims=True) + EPS) * subg_ref[...]
            o_ref[rows, cols] = jnp.transpose(y * (1.0 - lam_init)).astype(o_ref.dtype)
        return c

    lax.fori_loop(0, nq, q_body, 0)


def _diff_prompt(qn, kn, proj, diff_lambda, sub_g, lam_init):
    tp, width = qn.shape
    nb = tp // SEQ
    nh = width // DV_DIFF
    hpb = min(4, nh)
    tile = min(256, SEQ)
    w = hpb * DV_DIFF
    body = functools.partial(_diff_prompt_body, nq=SEQ // tile, tile=tile, n_heads=hpb,
                             lam_init=lam_init)
    return pl.pallas_call(
        body,
        grid=(nb, nh // hpb),
        in_specs=[pl.BlockSpec((SEQ, w), lambda b, h: (b, h)),
                  pl.BlockSpec((SEQ, w), lambda b, h: (b, h)),
                  pl.BlockSpec((1, SEQ, w), lambda b, h: (2, b, h)),
                  pl.BlockSpec(diff_lambda.shape, lambda b, h: (0, 0)),
                  pl.BlockSpec((DV_DIFF, 1), lambda b, h: (0, 0))],
        out_specs=pl.BlockSpec((SEQ, w), lambda b, h: (b, h)),
        out_shape=jax.ShapeDtypeStruct((tp, width), BF16),
        scratch_shapes=[pltpu.VMEM((SEQ, w), BF16),
                        pltpu.VMEM((hpb, SEQ // tile, DV_DIFF, tile), BF16),
                        pltpu.VMEM((hpb, SEQ // tile, DV_DIFF, tile), BF16),
                        pltpu.VMEM((hpb, DV_DIFF, 2 * tile), BF16),
                        pltpu.VMEM((hpb, 1, 2 * tile), F32), pltpu.VMEM((hpb, 1, 2 * tile), F32),
                        pltpu.VMEM((hpb, DV_DIFF, 2 * tile), F32)],
        compiler_params=_cp(("arbitrary", "arbitrary")),
        name="diff_prompt",
    )(qn, kn, proj, diff_lambda, sub_g[:, None])


def _diff_sample_body(pt_ref, q_ref, kn_ref, vn_ref, bias_ref, *rest, n_heads, n_steps, pages_per_step,
                      lam_init):
    kc_refs = rest[:pages_per_step]
    vc_refs = rest[pages_per_step:2 * pages_per_step]
    lam_ref, subg_ref, o_ref, m_scr, l_scr, acc_scr = rest[2 * pages_per_step:]
    s = pl.program_id(1)
    nq = 2 * DEC_SEQ

    @pl.when(s == 0)
    def _():
        m_scr[...] = jnp.full(m_scr.shape, -jnp.inf, F32)
        l_scr[...] = jnp.zeros_like(l_scr)
        acc_scr[...] = jnp.zeros_like(acc_scr)

    qz = q_ref[0]
    for kc_ref, vc_ref in zip(kc_refs, vc_refs):
        sc = _dot_nt(qz, kc_ref[0, 0].astype(BF16)) + bias_ref[...]
        m_old = m_scr[...]
        m_new = jnp.maximum(m_old, jnp.max(sc, axis=-1, keepdims=True))
        alpha = jnp.exp(m_old - m_new)
        p = jnp.exp(sc - m_new)
        l_scr[...] = alpha * l_scr[...] + jnp.sum(p, axis=-1, keepdims=True)
        acc_scr[...] = alpha * acc_scr[...] + _dot(p.astype(BF16), vc_ref[0, 0].astype(BF16))
        m_scr[...] = m_new

    @pl.when(s == n_steps - 1)
    def _():
        lam = _diff_lambda(lam_ref, lam_init)
        row = lax.broadcasted_iota(jnp.int32, (nq, ATT_BLK), 0) & (DEC_SEQ - 1)
        col = lax.broadcasted_iota(jnp.int32, (nq, ATT_BLK), 1)
        causal = col <= row
        pad = jnp.zeros((ATT_BLK - SROWS, LANES), F32)
        first_comp = lax.broadcasted_iota(jnp.int32, (nq, DV_DIFF), 0) < DEC_SEQ
        qf = qz.astype(F32)
        for h in range(n_heads):
            cols = slice(h * LANES, (h + 1) * LANES)
            rows = slice(h * nq, (h + 1) * nq)
            kblk = jnp.concatenate([kn_ref[:, cols], pad], axis=0).astype(BF16)
            vblk = jnp.concatenate([vn_ref[0, :, cols], pad], axis=0).astype(BF16)
            _, l, acc = _softmax_step(qf[rows].astype(BF16), kblk, vblk, m_scr[rows, :],
                                      l_scr[rows, :], acc_scr[rows, :], causal)
            o2 = acc / l
            o = o2 - lam * pltpu.roll(o2, DEC_SEQ, axis=0)
            y = o * lax.rsqrt(jnp.mean(o * o, axis=-1, keepdims=True) + EPS) * subg_ref[...]
            o_ref[:, cols] = jnp.where(first_comp, y * (1.0 - lam_init), 0.0)


def _diff_sample(qn_s, kn_s, proj_s, cache_k, cache_v, o, page_table, diff_lambda, sub_g, lam_init):
    sr, width = qn_s.shape
    nh = width // DV_DIFF
    nreq, n_pages = page_table.shape
    assert 2 * DEC_SEQ == SROWS
    pps = 4 if n_pages % 4 == 0 else 1
    n_steps = n_pages // pps
    nq = 2 * DEC_SEQ
    rows = nh * nq
    q4 = jnp.transpose(qn_s.reshape(nreq, SROWS, nh, DV_DIFF)[:, :DEC_SEQ], (0, 2, 1, 3))
    lane = jnp.arange(DV_DIFF)
    qz = jnp.stack([jnp.where(lane < HD_DIFF, q4, 0.0), jnp.where(lane >= HD_DIFF, q4, 0.0)], axis=2)
    qz = qz.reshape(nreq, rows, DV_DIFF).astype(BF16)
    same_head = (np.arange(PAGE_SIZE * nh)[None, :] % nh) == (np.arange(rows)[:, None] // nq)
    bias = jnp.asarray(np.where(same_head, 0.0, -np.inf), F32)
    body = functools.partial(_diff_sample_body, n_heads=nh, n_steps=n_steps, pages_per_step=pps,
                             lam_init=lam_init)
    page = lambda g: pl.BlockSpec((1, 1, PAGE_SIZE * nh, DV_DIFF),
                                  lambda b, s, pt: (o, pt[b, s * pps + g], 0, 0))
    return pl.pallas_call(
        body,
        grid_spec=pltpu.PrefetchScalarGridSpec(
            num_scalar_prefetch=1,
            grid=(nreq, n_steps),
            in_specs=[pl.BlockSpec((1, rows, DV_DIFF), lambda b, s, pt: (b, 0, 0)),
                      pl.BlockSpec((SROWS, width), lambda b, s, pt: (b, 0)),
                      pl.BlockSpec((1, SROWS, width), lambda b, s, pt: (2, b, 0)),
                      pl.BlockSpec(bias.shape, lambda b, s, pt: (0, 0))]
                     + [page(g) for g in range(pps)] + [page(g) for g in range(pps)]
                     + [pl.BlockSpec(diff_lambda.shape, lambda b, s, pt: (0, 0)),
                        pl.BlockSpec((1, DV_DIFF), lambda b, s, pt: (0, 0))],
            out_specs=pl.BlockSpec((SROWS, width), lambda b, s, pt: (b, 0)),
            scratch_shapes=[pltpu.VMEM((rows, 1), F32), pltpu.VMEM((rows, 1), F32),
                            pltpu.VMEM((rows, DV_DIFF), F32)],
        ),
        out_shape=jax.ShapeDtypeStruct((sr, width), F32),
        compiler_params=_cp(("arbitrary", "arbitrary")),
        name="diff_sample",
    )(page_table, qz, kn_s, proj_s, bias, *([cache_k] * pps), *([cache_v] * pps),
      diff_lambda, sub_g[None, :])


def _route(h, whi_ref, wlo_ref, b_ref):
    hi, lo = _split_bf16(h)
    logits = _dot(hi, whi_ref[...]) + _dot(lo, whi_ref[...]) + _dot(hi, wlo_ref[...]) + b_ref[...]
    lane = lax.broadcasted_iota(jnp.int32, logits.shape, 1).astype(F32)
    neg = -jnp.inf
    first = lambda hit: jnp.min(jnp.where(hit, lane, float(LANES)), axis=-1, keepdims=True)
    gl = jnp.where(lane < N_GROUPS, logits, neg)
    gmax = jnp.max(gl, axis=-1, keepdims=True)
    g_top = 1.0 / jnp.sum(jnp.exp(gl - gmax), axis=-1, keepdims=True)
    g_idx = first(gl == gmax)
    lo_lane = N_GROUPS + EXP_PER_GROUP * g_idx
    el = jnp.where(jnp.logical_and(lane >= lo_lane, lane < lo_lane + EXP_PER_GROUP), logits, neg)
    emax = jnp.max(el, axis=-1, keepdims=True)
    esum = jnp.sum(jnp.exp(el - emax), axis=-1, keepdims=True)
    l1 = first(el == emax)
    el2 = jnp.where(lane == l1, neg, el)
    e2max = jnp.max(el2, axis=-1, keepdims=True)
    l2 = first(el2 == e2max)
    p1 = 1.0 / esum
    p2 = jnp.exp(e2max - emax) / esum
    w1 = p1 / (p1 + p2) * g_top
    w2 = p2 / (p1 + p2) * g_top
    rec = jnp.where(lane == 0, l1 - N_GROUPS,
                    jnp.where(lane == 1, l2 - N_GROUPS,
                              jnp.where(lane == 2, w1, jnp.where(lane == 3, w2, 0.0))))
    return hi, rec


def _router_body(xp_ref, xs_ref, g_ref, sh_ref, sc_ref, shs_ref, scs_ref, whi_ref, wlo_ref, b_ref,
                 h_ref, rec_ref, *, n_p, tiles_per_batch):
    i = pl.program_id(0)
    tm = xp_ref.shape[0]
    sr = xs_ref.shape[0]

    @pl.when(i < n_p)
    def _():
        b = i // tiles_per_batch
        sh = sh_ref[0, 0, pl.ds(b, 1), :]
        sc = sc_ref[0, 0, pl.ds(b, 1), :]
        chunk = min(256, tm)

        def body(r, c):
            sl = pl.ds(pl.multiple_of(r * chunk, chunk), chunk)
            hi, rec = _route(_modulate(xp_ref[sl, :], g_ref[...], sh, sc), whi_ref, wlo_ref, b_ref)
            h_ref[sl, :] = hi
            rec_ref[sl, :] = rec
            return c

        lax.fori_loop(0, tm // chunk, body, 0)

    @pl.when(i == n_p)
    def _():
        hi, rec = _route(_mod_sample_tile(xs_ref, g_ref[...], shs_ref, scs_ref), whi_ref, wlo_ref, b_ref)
        h_ref[:sr, :] = hi
        rec_ref[:sr, :] = rec
        h_ref[sr:, :] = jnp.zeros((tm - sr, h_ref.shape[1]), h_ref.dtype)
        rec_ref[sr:, :] = jnp.zeros((tm - sr, LANES), F32)


def _router(xp, xs, g, mods, li, mod_s, w_rg, b_rg, w_re, b_re):
    tp, d = xp.shape
    sr = xs.shape[0]
    tm = _prompt_tile()
    n_p = tp // tm
    last = n_p - 1
    wr = jnp.concatenate([w_rg, jnp.moveaxis(w_re, 0, 1).reshape(d, N_EXPERTS),
                          jnp.zeros((d, LANES - N_GROUPS - N_EXPERTS), F32)], axis=1)
    whi = wr.astype(BF16)
    wlo = (wr - whi.astype(F32)).astype(BF16)
    bias = jnp.concatenate([b_rg, b_re.reshape(-1), jnp.zeros((LANES - N_GROUPS - N_EXPERTS,), F32)])[None]
    body = functools.partial(_router_body, n_p=n_p, tiles_per_batch=SEQ // tm)
    full = lambda r, c: pl.BlockSpec((r, c), lambda i: (0, 0))
    rows = tp + tm
    return pl.pallas_call(
        body,
        grid=(n_p + 1,),
        in_specs=[pl.BlockSpec((tm, d), lambda i: (jnp.minimum(i, last), 0)),
                  full(sr, d),
                  pl.BlockSpec((1, d), lambda i: (0, 0)),
                  pl.BlockSpec((1, 1, MOD_ROWS, d), lambda i: (li, 3, 0, 0)),
                  pl.BlockSpec((1, 1, MOD_ROWS, d), lambda i: (li, 4, 0, 0)),
                  full(sr, d), full(sr, d), full(d, LANES), full(d, LANES), full(1, LANES)],
        out_specs=[pl.BlockSpec((tm, d), lambda i: (i, 0)),
                   pl.BlockSpec((tm, LANES), lambda i: (i, 0))],
        out_shape=[jax.ShapeDtypeStruct((rows, d), BF16),
                   jax.ShapeDtypeStruct((rows, LANES), F32)],
        compiler_params=_cp(("arbitrary",)),
        name="router",
    )(xp, xs, g, mods, mods, mod_s[3], mod_s[4], whi, wlo, bias)


def _experts_body(te_ref, nv_ref, x_ref, wi_ref, wo_ref, y_ref, wi_scr, wo_scr):
    t = pl.program_id(0)
    f = wo_ref.shape[2]
    fresh = jnp.logical_or(t == 0, te_ref[t] != te_ref[jnp.maximum(t - 1, 0)])

    @pl.when(jnp.logical_and(t < nv_ref[0], fresh))
    def _():
        wi_scr[...] = wi_ref[0, 0].astype(BF16)
        wo_scr[...] = wo_ref[0, 0].astype(BF16)

    @pl.when(t < nv_ref[0])
    def _():
        hid = _dot(x_ref[...], wi_scr[...])
        gate = hid[:, f:]
        act = (gate * jax.nn.sigmoid(gate)) * hid[:, :f]
        y_ref[...] = _dot(act.astype(BF16), wo_scr[...]).astype(y_ref.dtype)

    @pl.when(t >= nv_ref[0])
    def _():
        y_ref[...] = jnp.zeros_like(y_ref)


def _experts(xs_sorted, tile_expert, n_valid, w_in, w_out, li):
    p, d = xs_sorted.shape
    f2 = w_in.shape[3]
    return pl.pallas_call(
        _experts_body,
        grid_spec=pltpu.PrefetchScalarGridSpec(
            num_scalar_prefetch=2,
            grid=(p // EXP_TILE,),
            in_specs=[pl.BlockSpec((EXP_TILE, d), lambda t, te, nv: (t, 0)),
                      pl.BlockSpec((1, 1, d, f2), lambda t, te, nv: (li, te[t], 0, 0)),
                      pl.BlockSpec((1, 1, f2 // 2, d), lambda t, te, nv: (li, te[t], 0, 0))],
            out_specs=pl.BlockSpec((EXP_TILE, d), lambda t, te, nv: (t, 0)),
            scratch_shapes=[pltpu.VMEM((d, f2), BF16), pltpu.VMEM((f2 // 2, d), BF16)],
        ),
        out_shape=jax.ShapeDtypeStruct((p, d), BF16),
        compiler_params=_cp(("arbitrary",)),
        name="experts",
    )(tile_expert, n_valid, xs_sorted, w_in, w_out)


def _combine_body(y1p_ref, y2p_ref, y1s_ref, y2s_ref, recp_ref, recs_ref, xp_ref, xs_ref,
                  gate_ref, gates_ref, op_ref, os_ref, *, n_p, tiles_per_batch):
    i = pl.program_id(0)

    def mix(y1, y2, rec):
        return rec[:, 2:3] * y1.astype(F32) + rec[:, 3:4] * y2.astype(F32)

    @pl.when(i < n_p)
    def _():
        b = i // tiles_per_batch
        gate = gate_ref[0, 0, pl.ds(b, 1), :]
        op_ref[...] = xp_ref[...] + gate * mix(y1p_ref[0], y2p_ref[0], recp_ref[...])

    @pl.when(i == n_p)
    def _():
        os_ref[...] = xs_ref[...] + gates_ref[...] * mix(y1s_ref[0], y2s_ref[0], recs_ref[...])


def _combine(yg, rec, xp, xs, mods, li, mod_s):
    tp, d = xp.shape
    sr = xs.shape[0]
    tm = min(512, SEQ)
    n_p = tp // tm
    last = n_p - 1
    stile = tp // sr
    body = functools.partial(_combine_body, n_p=n_p, tiles_per_batch=SEQ // tm)
    pmap = lambda i: (jnp.minimum(i, last), 0)
    return pl.pallas_call(
        body,
        grid=(n_p + 1,),
        in_specs=[pl.BlockSpec((1, tm, d), lambda i: (0, jnp.minimum(i, last), 0)),
                  pl.BlockSpec((1, tm, d), lambda i: (1, jnp.minimum(i, last), 0)),
                  pl.BlockSpec((1, sr, d), lambda i: (0, stile, 0)),
                  pl.BlockSpec((1, sr, d), lambda i: (1, stile, 0)),
                  pl.BlockSpec((tm, LANES), pmap),
                  pl.BlockSpec((sr, LANES), lambda i: (stile, 0)),
                  pl.BlockSpec((tm, d), pmap),
                  pl.BlockSpec((sr, d), lambda i: (0, 0)),
                  pl.BlockSpec((1, 1, MOD_ROWS, d), lambda i: (li, 5, 0, 0)),
                  pl.BlockSpec((sr, d), lambda i: (0, 0))],
        out_specs=[pl.BlockSpec((tm, d), pmap), pl.BlockSpec((sr, d), lambda i: (0, 0))],
        out_shape=[jax.ShapeDtypeStruct((tp, d), F32), jax.ShapeDtypeStruct((sr, d), F32)],
        compiler_params=_cp(("arbitrary",)),
        name="combine",
    )(yg, yg, yg, yg, rec, rec, xp, xs, mods, mod_s[5])


def _moe(xp, xs, g, mods, li, mod_s, w_rg, b_rg, w_re, b_re, w_in, w_out):
    tp, d = xp.shape
    sr = xs.shape[0]
    h_all, rec = _router(xp, xs, g, mods, li, mod_s, w_rg, b_rg, w_re, b_re)
    nt = tp + sr
    zero_row = nt
    row = jnp.arange(nt)
    real = jnp.logical_or(row < tp, (row - tp) % SROWS < DEC_SEQ)
    eid = jnp.where(real[:, None], rec[:nt, :2].astype(jnp.int32), N_EXPERTS)
    flat = eid.T.reshape(-1)
    onehot = (flat[:, None] == jnp.arange(N_EXPERTS)[None, :]).astype(jnp.int32)
    before = jnp.cumsum(onehot, axis=0) - onehot
    rank = jnp.sum(before * onehot, axis=1)
    count = jnp.sum(onehot, axis=0)
    padded = (count + EXP_TILE - 1) // EXP_TILE * EXP_TILE
    ends = jnp.cumsum(padded)
    start = ends - padded
    n_real = 2 * (tp + (sr // SROWS) * DEC_SEQ)
    p = (n_real + N_EXPERTS * (EXP_TILE - 1) + EXP_TILE - 1) // EXP_TILE * EXP_TILE
    is_real = flat < N_EXPERTS
    pos = jnp.where(is_real, start[jnp.minimum(flat, N_EXPERTS - 1)] + rank, p)
    src = jnp.full((p,), zero_row, jnp.int32).at[pos].set(jnp.tile(row, 2), mode="drop")
    tile_start = jnp.arange(p // EXP_TILE) * EXP_TILE
    tile_expert = jnp.minimum(jnp.sum((ends[None, :] <= tile_start[:, None]).astype(jnp.int32), axis=1),
                              N_EXPERTS - 1)
    n_valid = (ends[-1] // EXP_TILE).astype(jnp.int32)[None]
    xs_sorted = jnp.take(h_all, src, axis=0)
    ys = _experts(xs_sorted, tile_expert, n_valid, w_in, w_out, li)
    yg = jnp.take(ys, jnp.minimum(pos, p - 1).reshape(2, nt), axis=0)
    return _combine(yg, rec, xp, xs, mods, li, mod_s)


def kernel(x_prompt, x_sample, cache_sb_k, cache_sb_v, state_ret, cache_diff_k, cache_diff_v, page_table, c_prompt, c_sample, w_ada, b_ada, norm_mix, norm_ffn, w_in_even, w_out_even, ret_gn_gain, w_in_odd, w_out_odd, qk_gain, diff_lambda, diff_subln, w_router_group, b_router_group, w_router_expert, b_router_expert, w_expert_in, w_expert_out):
    nb, seq, d = x_prompt.shape
    nreq, dec_seq, _ = x_sample.shape
    n_pages = page_table.shape[1]
    past_len = n_pages * PAGE_SIZE
    tp = nb * seq
    sr = nreq * SROWS
    h_sb = cache_sb_k.shape[3]
    h_diff = cache_diff_k.shape[3]

    xp = x_prompt.reshape(tp, d)
    xs = jnp.pad(x_sample, ((0, 0), (0, SROWS - dec_seq), (0, 0))).reshape(sr, d)
    c_all = jnp.concatenate([c_prompt, c_sample, jnp.zeros((MOD_ROWS - nb - nreq, d), F32)], axis=0)
    mods = _ada(c_all, w_ada, b_ada)
    uu = _cumsum_mat(ATT_BLK)
    csk = cache_sb_k.reshape(cache_sb_k.shape[0], cache_sb_k.shape[1], PAGE_SIZE * h_sb, HD_SB)
    csv = cache_sb_v.reshape(csk.shape)
    cdk = cache_diff_k.reshape(cache_diff_k.shape[0], cache_diff_k.shape[1], PAGE_SIZE * h_diff, DV_DIFF)
    cdv = cache_diff_v.reshape(cdk.shape)
    pos_s = past_len + (jnp.arange(sr) % SROWS)

    def sample_rows(a, tail):
        return a.reshape(nreq, SROWS, *tail)[:, :dec_seq]

    sbk_p, sbv_p, sbk_s, sbv_s, ret_p, ret_s = [], [], [], [], [], []
    dk_p, dv_p, dk_s, dv_s = [], [], [], []
    for li in range(DEPTH):
        mod_s = [jnp.repeat(mods[li, k, nb:nb + nreq], SROWS, axis=0) for k in range(ADA_MOD)]
        if li % 2 == 0:
            e = li // 2
            w_in = w_in_even[e].astype(BF16)
            proj_p, proj_s = _inproj(xp, xs, norm_mix[li][None], mods, li, 0, 1, mod_s, w_in, h_sb * HD_SB)
            gain = ret_gn_gain[e][None, :]
            o_sb_p = _sb_prompt(proj_p, uu)
            o_r_p, s_p = _ret_prompt(proj_p, gain)
            o_sb_s = _sb_sample(proj_s, csk, csv, e, page_table, uu)
            o_r_s, s_s = _ret_sample(proj_s, gain, state_ret[e], past_len)
            xp, xs = _outproj((o_sb_p, o_r_p), (o_sb_s, o_r_s), w_out_even[e].astype(BF16),
                              xp, xs, mods, li, 2, mod_s)
            sbk_p.append(proj_p[1].reshape(nb, seq, h_sb, HD_SB))
            sbv_p.append(proj_p[2].reshape(nb, seq, h_sb, HD_SB))
            sbk_s.append(sample_rows(proj_s[1], (h_sb, HD_SB)))
            sbv_s.append(sample_rows(proj_s[2], (h_sb, HD_SB)))
            ret_p.append(s_p)
            ret_s.append(s_s)
        else:
            o = li // 2
            lam_init = 0.8 - 0.6 * math.exp(-0.3 * li)
            w_in = w_in_odd[o].astype(BF16)
            proj_p, proj_s = _inproj(xp, xs, norm_mix[li][None], mods, li, 0, 1, mod_s, w_in, h_diff * DV_DIFF)
            qn_p, kn_p = _diff_prep(proj_p, qk_gain[o], jnp.arange(seq), seq, BF16)
            qn_s, kn_s = _diff_prep(proj_s, qk_gain[o], pos_s, sr, F32)
            o_p = _diff_prompt(qn_p, kn_p, proj_p, diff_lambda[o], diff_subln[o], lam_init)
            o_s = _diff_sample(qn_s, kn_s, proj_s, cdk, cdv, o, page_table, diff_lambda[o],
                               diff_subln[o], lam_init)
            xp, xs = _outproj(o_p, o_s, w_out_odd[o].astype(BF16), xp, xs, mods, li, 2, mod_s)
            dk_p.append(kn_p.reshape(nb, seq, h_diff, DV_DIFF))
            dv_p.append(proj_p[2].reshape(nb, seq, h_diff, DV_DIFF))
            dk_s.append(sample_rows(kn_s, (h_diff, DV_DIFF)))
            dv_s.append(sample_rows(proj_s[2], (h_diff, DV_DIFF)))
        xp, xs = _moe(xp, xs, norm_ffn[li][None], mods, li, mod_s, w_router_group[li], b_router_group[li],
                      w_router_expert[li], b_router_expert[li], w_expert_in, w_expert_out)
    return (xp.reshape(nb, seq, d), sample_rows(xs, (d,)),
            jnp.stack(sbk_p), jnp.stack(sbv_p), jnp.stack(sbk_s), jnp.stack(sbv_s),
            jnp.stack(ret_p), jnp.stack(ret_s), jnp.stack(dk_p), jnp.stack(dv_p),
            jnp.stack(dk_s), jnp.stack(dv_s))
```

```python
import functools
import math

import jax
import jax.numpy as jnp
import numpy as np
from jax import lax
from jax.experimental import pallas as pl
from jax.experimental.pallas import tpu as pltpu

D_MODEL = 2048
BATCH = 4
SEQ = 2048
DEPTH = 4
DEC_BATCH = 8
DEC_SEQ = 4
PAGE_SIZE = 128
EPS = 1e-6
HD_SB = 128
DK_RET = 128
DV_RET = 256
RET_THETA = 10000.0
HD_DIFF = 64
DV_DIFF = 2 * HD_DIFF
ROPE_THETA = 500000.0
ROT_DIM = HD_DIFF // 4
N_GROUPS = 4
EXP_PER_GROUP = 4
N_EXPERTS = N_GROUPS * EXP_PER_GROUP
D_EXPERT = 512
ADA_MOD = 6

F32 = jnp.float32
BF16 = jnp.bfloat16

LANES = 128
SUBLANES = 8
MOD_ROWS = 16
SROWS = SUBLANES
ATT_BLK = 128
EXP_TILE = 256
SB_EXIT = -104.0
VMEM_LIMIT = 56 * 2**20


def _cp(sem, vmem=VMEM_LIMIT):
    return pltpu.CompilerParams(dimension_semantics=sem, vmem_limit_bytes=vmem)


def _dot(a, b):
    return jnp.dot(a, b, preferred_element_type=F32)


def _dot_nt(a, b):
    return lax.dot_general(a, b, (((1,), (1,)), ((), ())), preferred_element_type=F32)


def _split_bf16(x):
    hi = x.astype(BF16)
    lo = (x - hi.astype(F32)).astype(BF16)
    return hi, lo


def _modulate(x, g, shift, scale):
    y = x * lax.rsqrt(jnp.mean(x * x, axis=-1, keepdims=True) + EPS) * g
    return y * (1.0 + scale) + shift


def _sample_row_valid(rows):
    r = lax.broadcasted_iota(jnp.int32, (rows, 1), 0)
    return (r & (SROWS - 1)) < DEC_SEQ


def _ada_body(c_ref, w_ref, b_ref, o_ref):
    c = c_ref[...]
    a = (c * jax.nn.sigmoid(c)).astype(BF16)
    o_ref[0, 0] = _dot(a, w_ref[0].astype(BF16)) + b_ref[0]


def _ada(c_all, w_ada, b_ada):
    depth, d, _ = w_ada.shape
    tn = min(1024, d)
    nj = d // tn
    return pl.pallas_call(
        _ada_body,
        grid=(depth, ADA_MOD, nj),
        in_specs=[
            pl.BlockSpec((MOD_ROWS, d), lambda l, k, j: (0, 0)),
            pl.BlockSpec((1, d, tn), lambda l, k, j: (l, 0, k * nj + j)),
            pl.BlockSpec((1, 1, tn), lambda l, k, j: (l, 0, k * nj + j)),
        ],
        out_specs=pl.BlockSpec((1, 1, MOD_ROWS, tn), lambda l, k, j: (l, k, 0, j)),
        out_shape=jax.ShapeDtypeStruct((depth, ADA_MOD, MOD_ROWS, d), F32),
        compiler_params=_cp(("arbitrary",) * 3),
        name="ada",
    )(c_all, w_ada, b_ada.reshape(depth, 1, -1))


def _prompt_tile():
    return min(1024, SEQ)


def _mod_prompt_tile(x_ref, h_ref, g, sh, sc, rows):
    chunk = min(256, rows)

    def body(r, c):
        sl = pl.ds(pl.multiple_of(r * chunk, chunk), chunk)
        h_ref[sl, :] = _modulate(x_ref[sl, :], g, sh, sc).astype(h_ref.dtype)
        return c

    lax.fori_loop(0, rows // chunk, body, 0)


def _mod_sample_tile(xs_ref, g, shs_ref, scs_ref):
    h = _modulate(xs_ref[...], g, shs_ref[...], scs_ref[...])
    return jnp.where(_sample_row_valid(xs_ref.shape[0]), h, 0.0)


def _inproj_body(xp_ref, xs_ref, g_ref, sh_ref, sc_ref, shs_ref, scs_ref, w_ref,
                 op_ref, os_ref, hp_scr, hs_scr, *, n_p, tiles_per_batch):
    i = pl.program_id(0)
    j = pl.program_id(1)
    tm = xp_ref.shape[0]

    @pl.when(jnp.logical_and(i < n_p, j == 0))
    def _():
        b = i // tiles_per_batch
        _mod_prompt_tile(xp_ref, hp_scr, g_ref[...], sh_ref[0, 0, pl.ds(b, 1), :],
                         sc_ref[0, 0, pl.ds(b, 1), :], tm)

    @pl.when(jnp.logical_and(i == n_p, j == 0))
    def _():
        hs_scr[...] = _mod_sample_tile(xs_ref, g_ref[...], shs_ref, scs_ref).astype(BF16)

    @pl.when(i < n_p)
    def _():
        chunk = min(512, tm)

        def body(r, c):
            sl = pl.ds(pl.multiple_of(r * chunk, chunk), chunk)
            op_ref[0, sl, :] = _dot(hp_scr[sl, :], w_ref[...])
            return c

        lax.fori_loop(0, tm // chunk, body, 0)

    @pl.when(i == n_p)
    def _():
        os_ref[0] = _dot(hs_scr[...], w_ref[...])


def _inproj(xp, xs, g, mods, li, k_shift, k_scale, mod_s, w, group_w):
    tp, d = xp.shape
    sr = xs.shape[0]
    n = w.shape[1]
    tm = _prompt_tile()
    n_p = tp // tm
    tn = min(1024, group_w)
    nj = n // tn
    tpg = group_w // tn
    last = n_p - 1

    def p_out(i, j):
        jj = jnp.where(i == n_p, nj - 1, j)
        return (jj // tpg, jnp.minimum(i, last), jj % tpg)

    def s_out(i, j):
        jj = jnp.where(i == n_p, j, 0)
        return (jj // tpg, 0, jj % tpg)

    body = functools.partial(_inproj_body, n_p=n_p, tiles_per_batch=SEQ // tm)
    return pl.pallas_call(
        body,
        grid=(n_p + 1, nj),
        in_specs=[
            pl.BlockSpec((tm, d), lambda i, j: (jnp.minimum(i, last), 0)),
            pl.BlockSpec((sr, d), lambda i, j: (0, 0)),
            pl.BlockSpec((1, d), lambda i, j: (0, 0)),
            pl.BlockSpec((1, 1, MOD_ROWS, d), lambda i, j: (li, k_shift, 0, 0)),
            pl.BlockSpec((1, 1, MOD_ROWS, d), lambda i, j: (li, k_scale, 0, 0)),
            pl.BlockSpec((sr, d), lambda i, j: (0, 0)),
            pl.BlockSpec((sr, d), lambda i, j: (0, 0)),
            pl.BlockSpec((d, tn), lambda i, j: (0, j)),
        ],
        out_specs=[
            pl.BlockSpec((1, tm, tn), p_out),
            pl.BlockSpec((1, sr, tn), s_out),
        ],
        out_shape=[
            jax.ShapeDtypeStruct((n // group_w, tp, group_w), F32),
            jax.ShapeDtypeStruct((n // group_w, sr, group_w), F32),
        ],
        scratch_shapes=[pltpu.VMEM((tm, d), BF16), pltpu.VMEM((sr, d), BF16)],
        compiler_params=_cp(("arbitrary", "arbitrary")),
        name="inproj",
    )(xp, xs, g, mods, mods, mod_s[k_shift], mod_s[k_scale], w)


def _outproj_body(a1p_ref, a2p_ref, a1s_ref, a2s_ref, w1_ref, w2_ref, xp_ref, xs_ref,
                  gate_ref, gates_ref, op_ref, os_ref, *, n_p, tiles_per_batch):
    i = pl.program_id(0)
    tm = xp_ref.shape[0]

    @pl.when(i < n_p)
    def _():
        b = i // tiles_per_batch
        gate = gate_ref[0, 0, pl.ds(b, 1), :]
        chunk = min(512, tm)

        def body(r, c):
            sl = pl.ds(pl.multiple_of(r * chunk, chunk), chunk)
            y = _dot(a1p_ref[sl, :], w1_ref[...]) + _dot(a2p_ref[sl, :], w2_ref[...])
            op_ref[sl, :] = xp_ref[sl, :] + gate * y
            return c

        lax.fori_loop(0, tm // chunk, body, 0)

    @pl.when(i == n_p)
    def _():
        y = (_dot(a1s_ref[...].astype(BF16), w1_ref[...])
             + _dot(a2s_ref[...].astype(BF16), w2_ref[...]))
        os_ref[...] = xs_ref[...] + gates_ref[...] * y


def _outproj(ap, as_, w, xp, xs, mods, li, k_gate, mod_s):
    tp, d = xp.shape
    sr = xs.shape[0]
    tm = _prompt_tile()
    n_p = tp // tm
    tn = min(1024, d)
    nj = d // tn
    last = n_p - 1
    half = w.shape[0] // 2
    if isinstance(ap, tuple):
        a1p, a2p, a1s, a2s = ap[0], ap[1], as_[0], as_[1]
        c2 = 0
    else:
        a1p = a2p = ap
        a1s = a2s = as_
        c2 = 1

    def pin(c):
        return lambda i, j: (jnp.minimum(i, last), c)

    def p_out(i, j):
        return (jnp.minimum(i, last), jnp.where(i == n_p, nj - 1, j))

    def s_out(i, j):
        return (0, jnp.where(i == n_p, j, 0))

    body = functools.partial(_outproj_body, n_p=n_p, tiles_per_batch=SEQ // tm)
    return pl.pallas_call(
        body,
        grid=(n_p + 1, nj),
        in_specs=[
            pl.BlockSpec((tm, half), pin(0)),
            pl.BlockSpec((tm, half), pin(c2)),
            pl.BlockSpec((sr, half), lambda i, j: (0, 0)),
            pl.BlockSpec((sr, half), lambda i, j: (0, c2)),
            pl.BlockSpec((half, tn), lambda i, j: (0, j)),
            pl.BlockSpec((half, tn), lambda i, j: (1, j)),
            pl.BlockSpec((tm, tn), lambda i, j: (jnp.minimum(i, last), j)),
            pl.BlockSpec((sr, tn), lambda i, j: (0, j)),
            pl.BlockSpec((1, 1, MOD_ROWS, tn), lambda i, j: (li, k_gate, 0, j)),
            pl.BlockSpec((sr, tn), lambda i, j: (0, j)),
        ],
        out_specs=[pl.BlockSpec((tm, tn), p_out), pl.BlockSpec((sr, tn), s_out)],
        out_shape=[jax.ShapeDtypeStruct((tp, d), F32), jax.ShapeDtypeStruct((sr, d), F32)],
        compiler_params=_cp(("arbitrary", "arbitrary")),
        name="outproj",
    )(a1p, a2p, a1s, a2s, w, w, xp, xs, mods, mod_s[k_gate])


def _cumsum_mat(n):
    j = np.arange(n)[:, None]
    s = np.arange(n)[None, :]
    blk = np.concatenate([(j >= s).astype(np.float32), np.ones((n, n), np.float32)], axis=1)
    return jnp.asarray(np.concatenate([blk, blk], axis=0), dtype=BF16)


def _sb_block(qb, kblk, vblk, uu, carry, mask, scale):
    n = kblk.shape[0]
    z = _dot_nt(qb, kblk) * scale
    lk = -(jnp.maximum(z, 0.0) + jnp.log1p(jnp.exp(-jnp.abs(z))))
    if mask is not None:
        lk = jnp.where(mask, lk, 0.0)
    hi, lo = _split_bf16(lk)
    incl = [None] * (n // ATT_BLK)
    for g in reversed(range(n // ATT_BLK)):
        cols = slice(g * ATT_BLK, (g + 1) * ATT_BLK)
        r = _dot(jnp.concatenate([hi[:, cols], lo[:, cols]], axis=1), uu)
        incl[g] = r[:, :ATT_BLK] + carry
        carry = carry + r[:, ATT_BLK:]
    a = jnp.exp(z + jnp.concatenate(incl, axis=1))
    if mask is not None:
        a = jnp.where(mask, a, 0.0)
    return carry, _dot(a.astype(BF16), vblk)


def _sb_prompt_body(q_ref, k_ref, v_ref, uu_ref, o_ref, kb_scr, vb_scr, carry_scr, acc_scr,
                    *, nq, tile, n_heads, scale):
    kb_scr[...] = k_ref[0].astype(BF16)
    vb_scr[...] = v_ref[0].astype(BF16)
    row = lax.broadcasted_iota(jnp.int32, (tile, tile), 0)
    col = lax.broadcasted_iota(jnp.int32, (tile, tile), 1)
    strict = col < row
    uu = uu_ref[...]
    head_cols = [slice(h * HD_SB, (h + 1) * HD_SB) for h in range(n_heads)]

    def q_body(qi, c):
        rows = pl.ds(pl.multiple_of(qi * tile, tile), tile)

        def blocks(rk, first):
            mx = None
            for h, cols in enumerate(head_cols):
                qb = q_ref[0, rows, cols].astype(BF16)
                carry0 = jnp.zeros((tile, ATT_BLK), F32) if first else carry_scr[h]
                carry, o = _sb_block(qb, kb_scr[rk, cols], vb_scr[rk, cols], uu, carry0,
                                     strict if first else None, scale)
                carry_scr[h] = carry
                acc_scr[h] = o if first else acc_scr[h] + o
                top = jnp.max(carry)
                mx = top if mx is None else jnp.maximum(mx, top)
            return mx

        def cond(st):
            kb, mx = st
            return jnp.logical_and(kb >= 0, mx > SB_EXIT)

        def body(st):
            kb, _ = st
            return kb - 1, blocks(pl.ds(pl.multiple_of(kb * tile, tile), tile), False)

        lax.while_loop(cond, body, (qi - 1, blocks(rows, True)))
        for h, cols in enumerate(head_cols):
            o_ref[rows, cols] = acc_scr[h].astype(o_ref.dtype)
        return c

    lax.fori_loop(0, nq, q_body, 0)


def _sb_prompt(proj, uu):
    _, tp, width = proj.shape
    nb = tp // SEQ
    nh = width // HD_SB
    hpb = min(2, nh)
    tile = min(256, SEQ)
    w = hpb * HD_SB
    body = functools.partial(_sb_prompt_body, nq=SEQ // tile, tile=tile, n_heads=hpb,
                             scale=HD_SB ** -0.5)
    spec = lambda g: pl.BlockSpec((1, SEQ, w), lambda b, h: (g, b, h))
    return pl.pallas_call(
        body,
        grid=(nb, nh // hpb),
        in_specs=[spec(0), spec(1), spec(2),
                  pl.BlockSpec((2 * ATT_BLK, 2 * ATT_BLK), lambda b, h: (0, 0))],
        out_specs=pl.BlockSpec((SEQ, w), lambda b, h: (b, h)),
        out_shape=jax.ShapeDtypeStruct((tp, width), BF16),
        scratch_shapes=[pltpu.VMEM((SEQ, w), BF16), pltpu.VMEM((SEQ, w), BF16),
                        pltpu.VMEM((hpb, tile, ATT_BLK), F32), pltpu.VMEM((hpb, tile, HD_SB), F32)],
        compiler_params=_cp(("arbitrary", "arbitrary")),
        name="sb_prompt",
    )(proj, proj, proj, uu)


def _sb_sample_body(pt_ref, q_ref, kn_ref, vn_ref, kc_hbm, vc_hbm, uu_ref, o_ref,
                    kbuf, vbuf, sem, carry_scr, acc_scr, *, layer, n_heads, n_pages, scale):
    b = pl.program_id(0)
    blk = ATT_BLK
    uu = uu_ref[...]
    row = lax.broadcasted_iota(jnp.int32, (SROWS, blk), 0)
    col = lax.broadcasted_iota(jnp.int32, (SROWS, blk), 1)
    real_row = row < DEC_SEQ
    strict = col < row
    pad = jnp.zeros((blk - SROWS, HD_SB), F32)

    def blocks(first):
        mx = None
        for h in range(n_heads):
            cols = slice(h * HD_SB, (h + 1) * HD_SB)
            rows = slice(h * SROWS, (h + 1) * SROWS)
            qb = q_ref[0, :, cols].astype(BF16)
            if first:
                kblk = jnp.concatenate([kn_ref[0, :, cols], pad], axis=0).astype(BF16)
                vblk = jnp.concatenate([vn_ref[0, :, cols], pad], axis=0).astype(BF16)
                carry0 = jnp.zeros((SROWS, blk), F32)
            else:
                kblk = kbuf[pl.ds(h, PAGE_SIZE, stride=n_heads), :].astype(BF16)
                vblk = vbuf[pl.ds(h, PAGE_SIZE, stride=n_heads), :].astype(BF16)
                carry0 = carry_scr[rows, :]
            carry, o = _sb_block(qb, kblk, vblk, uu, carry0, strict if first else None, scale)
            carry_scr[rows, :] = carry
            acc_scr[rows, :] = o if first else acc_scr[rows, :] + o
            top = jnp.max(jnp.where(real_row, carry, -jnp.inf))
            mx = top if mx is None else jnp.maximum(mx, top)
        return mx

    def cond(st):
        p, mx = st
        return jnp.logical_and(p >= 0, mx > SB_EXIT)

    def body(st):
        p, _ = st
        page = pt_ref[b, p]
        ck = pltpu.make_async_copy(kc_hbm.at[layer, page], kbuf, sem.at[0])
        cv = pltpu.make_async_copy(vc_hbm.at[layer, page], vbuf, sem.at[1])
        ck.start()
        cv.start()
        ck.wait()
        cv.wait()
        return p - 1, blocks(False)

    lax.while_loop(cond, body, (n_pages - 1, blocks(True)))
    for h in range(n_heads):
        o_ref[:, h * HD_SB:(h + 1) * HD_SB] = acc_scr[h * SROWS:(h + 1) * SROWS, :]


def _sb_sample(proj_s, cache_k, cache_v, e, page_table, uu):
    _, sr, width = proj_s.shape
    nh = width // HD_SB
    nreq, n_pages = page_table.shape
    body = functools.partial(_sb_sample_body, layer=e, n_heads=nh, n_pages=n_pages,
                             scale=HD_SB ** -0.5)
    spec = lambda g: pl.BlockSpec((1, SROWS, width), lambda b, pt: (g, b, 0))
    return pl.pallas_call(
        body,
        grid_spec=pltpu.PrefetchScalarGridSpec(
            num_scalar_prefetch=1,
            grid=(nreq,),
            in_specs=[spec(0), spec(1), spec(2),
                      pl.BlockSpec(memory_space=pl.ANY),
                      pl.BlockSpec(memory_space=pl.ANY),
                      pl.BlockSpec((2 * ATT_BLK, 2 * ATT_BLK), lambda b, pt: (0, 0))],
            out_specs=pl.BlockSpec((SROWS, width), lambda b, pt: (b, 0)),
            scratch_shapes=[pltpu.VMEM((PAGE_SIZE * nh, HD_SB), F32),
                            pltpu.VMEM((PAGE_SIZE * nh, HD_SB), F32),
                            pltpu.SemaphoreType.DMA((2,)),
                            pltpu.VMEM((nh * SROWS, ATT_BLK), F32),
                            pltpu.VMEM((nh * SROWS, HD_SB), F32)],
        ),
        out_shape=jax.ShapeDtypeStruct((sr, width), F32),
        compiler_params=_cp(("arbitrary",)),
        name="sb_sample",
    )(page_table, proj_s, proj_s, proj_s, cache_k, cache_v, uu)


def _ret_tables(pos, chunk, n_heads):
    half = DK_RET // 2
    inv = np.float32(RET_THETA) ** (-np.arange(half, dtype=np.float32) / half)
    ang = pos.astype(F32)[:, None] * jnp.asarray(inv)[None, :]
    cos = jnp.concatenate([jnp.cos(ang), jnp.cos(ang)], axis=1)
    sin = jnp.concatenate([-jnp.sin(ang), jnp.sin(ang)], axis=1)
    log_g = jnp.log1p(-(2.0 ** (-5.0 - jnp.arange(n_heads, dtype=F32))))
    i = jnp.arange(chunk, dtype=F32)
    return cos, sin, log_g, i


def _rope_half(x, cos, sin):
    return x * cos + pltpu.roll(x, x.shape[-1] // 2, axis=1) * sin


def _head_ln_gate(o, gain, g):
    mu = jnp.mean(o, axis=-1, keepdims=True)
    d = o - mu
    var = jnp.mean(d * d, axis=-1, keepdims=True)
    return d * lax.rsqrt(var + EPS) * gain * (g * jax.nn.sigmoid(g))


def _ret_prompt_body(q_ref, k_ref, v_ref, g_ref, cos_ref, sin_ref, dec_ref, qd_ref, kd_ref,
                     gc_ref, gain_ref, o_ref, s_ref, s_scr, *, n_chunks):
    c = ATT_BLK
    s_scr[...] = jnp.zeros_like(s_scr)
    decay = dec_ref[0]
    qd = qd_ref[0]
    kd = kd_ref[0]
    gc = gc_ref[0, 0:1, :1]
    gain = gain_ref[...]

    def body(n, carry):
        rows = pl.ds(pl.multiple_of(n * c, c), c)
        cos = cos_ref[rows, :]
        sin = sin_ref[rows, :]
        q = _rope_half(q_ref[0, rows, :], cos, sin)
        k = _rope_half(k_ref[0, rows, :], cos, sin) * (DK_RET ** -0.5)
        vb = v_ref[0, rows, :].astype(BF16)
        s_prev = s_scr[...]
        scores = _dot_nt(q.astype(BF16), k.astype(BF16)) * decay
        o = _dot(scores.astype(BF16), vb) + _dot((q * qd).astype(BF16), s_prev.astype(BF16))
        kv = _dot(jnp.transpose(k * kd).astype(BF16), vb)
        s_scr[...] = gc * s_prev + kv
        o_ref[rows, :] = _head_ln_gate(o, gain, g_ref[0, rows, :]).astype(o_ref.dtype)
        return carry

    lax.fori_loop(0, n_chunks, body, 0)
    s_ref[0, 0] = s_scr[...]


def _ret_prompt(proj, gain):
    _, tp, width = proj.shape
    nb = tp // SEQ
    nh = width // DV_RET
    c = ATT_BLK
    cos, sin, log_g, i = _ret_tables(jnp.arange(SEQ), c, nh)
    diff = i[:, None] - i[None, :]
    decay = jnp.where(diff >= 0, jnp.exp(jnp.maximum(diff, 0.0)[None] * log_g[:, None, None]), 0.0)
    ones = jnp.ones((1, 1, DK_RET), F32)
    qd = jnp.exp((i + 1)[None, :, None] * log_g[:, None, None]) * ones
    kd = jnp.exp((c - 1 - i)[None, :, None] * log_g[:, None, None]) * ones
    gc = jnp.exp(c * log_g)[:, None, None] * jnp.ones((1, SUBLANES, LANES), F32)
    body = functools.partial(_ret_prompt_body, n_chunks=SEQ // c)
    tab = lambda: pl.BlockSpec((SEQ, DK_RET), lambda b, h: (0, 0))
    per_head = lambda r, w: pl.BlockSpec((1, r, w), lambda b, h: (h, 0, 0))
    return pl.pallas_call(
        body,
        grid=(nb, nh),
        in_specs=[
            pl.BlockSpec((1, SEQ, DK_RET), lambda b, h: (3, b, h)),
            pl.BlockSpec((1, SEQ, DK_RET), lambda b, h: (3, b, nh + h)),
            pl.BlockSpec((1, SEQ, DV_RET), lambda b, h: (4, b, h)),
            pl.BlockSpec((1, SEQ, DV_RET), lambda b, h: (5, b, h)),
            tab(), tab(),
            per_head(c, c), per_head(c, DK_RET), per_head(c, DK_RET), per_head(SUBLANES, LANES),
            pl.BlockSpec((1, DV_RET), lambda b, h: (0, h)),
        ],
        out_specs=[pl.BlockSpec((SEQ, DV_RET), lambda b, h: (b, h)),
                   pl.BlockSpec((1, 1, DK_RET, DV_RET), lambda b, h: (b, h, 0, 0))],
        out_shape=[jax.ShapeDtypeStruct((tp, width), BF16),
                   jax.ShapeDtypeStruct((nb, nh, DK_RET, DV_RET), F32)],
        scratch_shapes=[pltpu.VMEM((DK_RET, DV_RET), F32)],
        compiler_params=_cp(("arbitrary", "arbitrary")),
        name="ret_prompt",
    )(proj, proj, proj, proj, cos, sin, decay, qd, kd, gc, gain)


def _ret_sample_body(q_ref, k_ref, v_ref, g_ref, cos_ref, sin_ref, dec_ref, qd_ref, kd_ref,
                     gc_ref, gain_ref, s0_ref, o_ref, s_ref, *, n_req):
    sr = q_ref.shape[1]
    cos = cos_ref[...]
    sin = sin_ref[...]
    q = _rope_half(q_ref[0], cos, sin)
    k = _rope_half(k_ref[0], cos, sin) * (DK_RET ** -0.5)
    v = v_ref[0]
    vb = v.astype(BF16)
    scores = _dot_nt(q.astype(BF16), k.astype(BF16)) * dec_ref[0]
    o = _dot(scores.astype(BF16), vb)
    qdec = q * qd_ref[0]
    kdec = k * kd_ref[0]
    gc = gc_ref[0, 0:1, :1]
    row = lax.broadcasted_iota(jnp.int32, (sr, 1), 0)
    for b in range(n_req):
        mine = jnp.logical_and(row >= b * SROWS, row < (b + 1) * SROWS)
        s0 = s0_ref[b, 0]
        o = o + _dot(jnp.where(mine, qdec, 0.0).astype(BF16), s0.astype(BF16))
        kb = jnp.transpose(jnp.where(mine, kdec, 0.0)).astype(BF16)
        s_ref[b, 0] = gc * s0 + _dot(kb, vb)
    o_ref[...] = _head_ln_gate(o, gain_ref[...], g_ref[0])


def _ret_sample(proj_s, gain, state0, past_len):
    _, sr, width = proj_s.shape
    nh = width // DV_RET
    nreq = sr // SROWS
    t = jnp.arange(sr) % SROWS
    cos, sin, log_g, _ = _ret_tables(past_len + t, DEC_SEQ, nh)
    tf = t.astype(F32)
    diff = tf[:, None] - tf[None, :]
    same = (jnp.arange(sr)[:, None] // SROWS) == (jnp.arange(sr)[None, :] // SROWS)
    real = (t < DEC_SEQ)
    ok = same & (diff >= 0) & real[:, None] & real[None, :]
    decay = jnp.where(ok[None], jnp.exp(jnp.maximum(diff, 0.0)[None] * log_g[:, None, None]), 0.0)
    ones = jnp.ones((1, 1, DK_RET), F32)
    qd = jnp.exp((tf + 1)[None, :, None] * log_g[:, None, None]) * ones
    kd = jnp.where(real[None, :, None],
                   jnp.exp((DEC_SEQ - 1 - tf)[None, :, None] * log_g[:, None, None]), 0.0) * ones
    gc = jnp.exp(DEC_SEQ * log_g)[:, None, None] * jnp.ones((1, SUBLANES, LANES), F32)
    body = functools.partial(_ret_sample_body, n_req=nreq)
    tab = lambda: pl.BlockSpec((sr, DK_RET), lambda h: (0, 0))
    per_head = lambda r, w: pl.BlockSpec((1, r, w), lambda h: (h, 0, 0))
    st = pl.BlockSpec((nreq, 1, DK_RET, DV_RET), lambda h: (0, h, 0, 0))
    return pl.pallas_call(
        body,
        grid=(nh,),
        in_specs=[
            pl.BlockSpec((1, sr, DK_RET), lambda h: (3, 0, h)),
            pl.BlockSpec((1, sr, DK_RET), lambda h: (3, 0, nh + h)),
            pl.BlockSpec((1, sr, DV_RET), lambda h: (4, 0, h)),
            pl.BlockSpec((1, sr, DV_RET), lambda h: (5, 0, h)),
            tab(), tab(),
            per_head(sr, sr), per_head(sr, DK_RET), per_head(sr, DK_RET), per_head(SUBLANES, LANES),
            pl.BlockSpec((1, DV_RET), lambda h: (0, h)),
            st,
        ],
        out_specs=[pl.BlockSpec((sr, DV_RET), lambda h: (0, h)), st],
        out_shape=[jax.ShapeDtypeStruct((sr, width), F32),
                   jax.ShapeDtypeStruct(state0.shape, F32)],
        compiler_params=_cp(("arbitrary",)),
        name="ret_sample",
    )(proj_s, proj_s, proj_s, proj_s, cos, sin, decay, qd, kd, gc, gain, state0)


def _diff_rope_tables(pos):
    half = ROT_DIM // 2
    inv = np.float32(ROPE_THETA) ** (-np.arange(half, dtype=np.float32) / half)
    ang = pos.astype(F32)[:, None] * jnp.asarray(inv)[None, :]
    cos, sin = jnp.cos(ang), jnp.sin(ang)
    n = pos.shape[0]
    rest = HD_DIFF - ROT_DIM
    c = jnp.concatenate([cos, cos, jnp.ones((n, rest), F32)], axis=1)
    s1 = jnp.concatenate([-sin, jnp.zeros((n, half + rest), F32)], axis=1)
    s2 = jnp.concatenate([jnp.zeros((n, half), F32), sin, jnp.zeros((n, rest), F32)], axis=1)
    two = lambda a: jnp.concatenate([a, a], axis=1)
    return two(c), two(s1), two(s2)


def _seg_mean_mat():
    a = np.arange(2 * HD_DIFF)
    return jnp.asarray((a[:, None] // HD_DIFF == a[None, :] // HD_DIFF).astype(np.float32) / HD_DIFF,
                       dtype=BF16)


def _qk_norm_rope(x, gain, c, s1, s2, seg):
    hi, lo = _split_bf16(x * x)
    ms = _dot(hi, seg) + _dot(lo, seg)
    y = x * lax.rsqrt(ms + EPS) * gain
    half = ROT_DIM // 2
    return y * c + pltpu.roll(y, LANES - half, axis=1) * s1 + pltpu.roll(y, half, axis=1) * s2


def _diff_prep_body(q_ref, k_ref, qg_ref, kg_ref, c_ref, s1_ref, s2_ref, seg_ref, qo_ref, ko_ref,
                    *, n_heads, q_scale):
    c, s1, s2, seg = c_ref[...], s1_ref[...], s2_ref[...], seg_ref[...]
    for h in range(n_heads):
        cols = slice(h * LANES, (h + 1) * LANES)
        q = _qk_norm_rope(q_ref[0, :, cols], qg_ref[...], c, s1, s2, seg)
        qo_ref[:, cols] = (q * q_scale).astype(qo_ref.dtype)
        ko_ref[:, cols] = _qk_norm_rope(k_ref[0, :, cols], kg_ref[...], c, s1, s2, seg)


def _diff_prep(proj, qk_gain, pos, rows_per_pos_table, q_dtype):
    _, rows, width = proj.shape
    nh = width // LANES
    tm = min(256, rows)
    c, s1, s2 = _diff_rope_tables(pos)
    nt = rows_per_pos_table // tm
    qg = jnp.tile(qk_gain[0], 2)[None, :]
    kg = jnp.tile(qk_gain[1], 2)[None, :]
    body = functools.partial(_diff_prep_body, n_heads=nh, q_scale=HD_DIFF ** -0.5 * math.log2(math.e))
    tab = lambda: pl.BlockSpec((tm, LANES), lambda i: (i % nt, 0))
    vec = lambda: pl.BlockSpec((1, LANES), lambda i: (0, 0))
    return pl.pallas_call(
        body,
        grid=(rows // tm,),
        in_specs=[pl.BlockSpec((1, tm, width), lambda i: (0, i, 0)),
                  pl.BlockSpec((1, tm, width), lambda i: (1, i, 0)),
                  vec(), vec(), tab(), tab(), tab(),
                  pl.BlockSpec((LANES, LANES), lambda i: (0, 0))],
        out_specs=[pl.BlockSpec((tm, width), lambda i: (i, 0)),
                   pl.BlockSpec((tm, width), lambda i: (i, 0))],
        out_shape=[jax.ShapeDtypeStruct((rows, width), q_dtype),
                   jax.ShapeDtypeStruct((rows, width), F32)],
        compiler_params=_cp(("arbitrary",)),
        name="diff_prep",
    )(proj, proj, qg, kg, c, s1, s2, _seg_mean_mat())


def _diff_lambda(lam_ref, lam_init):
    lv = lam_ref[...]
    a = jnp.sum(lv[0:1] * lv[1:2], axis=-1, keepdims=True)
    b = jnp.sum(lv[2:3] * lv[3:4], axis=-1, keepdims=True)
    return jnp.exp(a) - jnp.exp(b) + lam_init


def _stack_components(q):
    lane = lax.broadcasted_iota(jnp.int32, q.shape, 1)
    zero = jnp.zeros_like(q)
    return jnp.concatenate([jnp.where(lane < HD_DIFF, q, zero), jnp.where(lane >= HD_DIFF, q, zero)],
                           axis=0)


def _softmax_step(qz, kblk, vblk, m, l, acc, mask):
    s = _dot_nt(qz, kblk)
    if mask is not None:
        s = jnp.where(mask, s, -jnp.inf)
    m_new = jnp.maximum(m, jnp.max(s, axis=-1, keepdims=True))
    alpha = jnp.exp2(m - m_new)
    p = jnp.exp2(s - m_new)
    l = alpha * l + jnp.sum(p, axis=-1, keepdims=True)
    acc = alpha * acc + _dot(p.astype(BF16), vblk)
    return m_new, l, acc


def _diff_finish(l, acc, n, lam, sub_g, out_scale):
    o = acc[:n] / l[:n] - lam * (acc[n:] / l[n:])
    y = o * lax.rsqrt(jnp.mean(o * o, axis=-1, keepdims=True) + EPS) * sub_g
    return y * out_scale


def _diff_prompt_body(q_ref, k_ref, v_ref, lam_ref, subg_ref, o_ref, kb_scr, vt_scr, qt_scr, qz_scr,
                      m_scr, l_scr, acc_scr, *, nq, tile, n_heads, lam_init):
    kb_scr[...] = k_ref[...].astype(BF16)
    head_cols = [slice(h * DV_DIFF, (h + 1) * DV_DIFF) for h in range(n_heads)]
    for h, cols in enumerate(head_cols):
        for j in range(nq):
            blk = slice(j * tile, (j + 1) * tile)
            vt_scr[h, j] = jnp.transpose(v_ref[0, blk, cols]).astype(BF16)
            qt_scr[h, j] = jnp.transpose(q_ref[blk, cols].astype(F32)).astype(BF16)
    lam = _diff_lambda(lam_ref, lam_init)
    key = lax.broadcasted_iota(jnp.int32, (tile, 2 * tile), 0)
    qry = lax.broadcasted_iota(jnp.int32, (tile, 2 * tile), 1) & (tile - 1)
    causal = key <= qry
    first_comp = lax.broadcasted_iota(jnp.int32, (DV_DIFF, tile), 0) < HD_DIFF

    def steps(kb, rk, mask):
        for h, cols in enumerate(head_cols):
            s = _dot(kb_scr[rk, cols], qz_scr[h])
            if mask is not None:
                s = jnp.where(mask, s, -jnp.inf)
            m_old = m_scr[h]
            m_new = jnp.maximum(m_old, jnp.max(s, axis=0, keepdims=True))
            alpha = jnp.exp2(m_old - m_new)
            p = jnp.exp2(s - m_new)
            l_scr[h] = alpha * l_scr[h] + jnp.sum(p, axis=0, keepdims=True)
            acc_scr[h] = alpha * acc_scr[h] + _dot(vt_scr[h, kb], p.astype(BF16))
            m_scr[h] = m_new

    def q_body(qi, c):
        rows = pl.ds(pl.multiple_of(qi * tile, tile), tile)
        for h in range(n_heads):
            qt = qt_scr[h, qi]
            zero = jnp.zeros_like(qt)
            qz_scr[h] = jnp.concatenate([jnp.where(first_comp, qt, zero),
                                         jnp.where(first_comp, zero, qt)], axis=1)
        m_scr[...] = jnp.full(m_scr.shape, -jnp.inf, F32)
        l_scr[...] = jnp.zeros_like(l_scr)
        acc_scr[...] = jnp.zeros_like(acc_scr)

        def kv_body(kb, c2):
            steps(kb, pl.ds(pl.multiple_of(kb * tile, tile), tile), None)
            return c2

        lax.fori_loop(0, qi, kv_body, 0)
        steps(qi, rows, causal)
        for h, cols in enumerate(head_cols):
            ot = acc_scr[h] / l_scr[h]
            d = ot[:, :tile] - lam * ot[:, tile:]
            y = d * lax.rsqrt(jnp.mean(d * d, axis=0, keepdims=True) + EPS) * subg_ref[...]
            o_ref[rows, cols] = jnp.transpose(y * (1.0 - lam_init)).astype(o_ref.dtype)
        return c

    lax.fori_loop(0, nq, q_body, 0)


def _diff_prompt(qn, kn, proj, diff_lambda, sub_g, lam_init):
    tp, width = qn.shape
    nb = tp // SEQ
    nh = width // DV_DIFF
    hpb = min(4, nh)
    tile = min(256, SEQ)
    w = hpb * DV_DIFF
    body = functools.partial(_diff_prompt_body, nq=SEQ // tile, tile=tile, n_heads=hpb,
                             lam_init=lam_init)
    return pl.pallas_call(
        body,
        grid=(nb, nh // hpb),
        in_specs=[pl.BlockSpec((SEQ, w), lambda b, h: (b, h)),
                  pl.BlockSpec((SEQ, w), lambda b, h: (b, h)),
                  pl.BlockSpec((1, SEQ, w), lambda b, h: (2, b, h)),
                  pl.BlockSpec(diff_lambda.shape, lambda b, h: (0, 0)),
                  pl.BlockSpec((DV_DIFF, 1), lambda b, h: (0, 0))],
        out_specs=pl.BlockSpec((SEQ, w), lambda b, h: (b, h)),
        out_shape=jax.ShapeDtypeStruct((tp, width), BF16),
        scratch_shapes=[pltpu.VMEM((SEQ, w), BF16),
                        pltpu.VMEM((hpb, SEQ // tile, DV_DIFF, tile), BF16),
                        pltpu.VMEM((hpb, SEQ // tile, DV_DIFF, tile), BF16),
                        pltpu.VMEM((hpb, DV_DIFF, 2 * tile), BF16),
                        pltpu.VMEM((hpb, 1, 2 * tile), F32), pltpu.VMEM((hpb, 1, 2 * tile), F32),
                        pltpu.VMEM((hpb, DV_DIFF, 2 * tile), F32)],
        compiler_params=_cp(("arbitrary", "arbitrary")),
        name="diff_prompt",
    )(qn, kn, proj, diff_lambda, sub_g[:, None])


def _diff_sample_body(pt_ref, q_ref, kn_ref, vn_ref, bias_ref, *rest, n_heads, n_steps, pages_per_step,
                      lam_init):
    kc_refs = rest[:pages_per_step]
    vc_refs = rest[pages_per_step:2 * pages_per_step]
    lam_ref, subg_ref, o_ref, m_scr, l_scr, acc_scr = rest[2 * pages_per_step:]
    s = pl.program_id(1)
    nq = 2 * DEC_SEQ

    @pl.when(s == 0)
    def _():
        m_scr[...] = jnp.full(m_scr.shape, -jnp.inf, F32)
        l_scr[...] = jnp.zeros_like(l_scr)
        acc_scr[...] = jnp.zeros_like(acc_scr)

    qz = q_ref[0]
    scores = [_dot_nt(qz, kc_ref[0, 0].astype(BF16)) + bias_ref[...] for kc_ref in kc_refs]
    m_old = m_scr[...]
    m_new = m_old
    for sc in scores:
        m_new = jnp.maximum(m_new, jnp.max(sc, axis=-1, keepdims=True))
    alpha = jnp.exp2(m_old - m_new)
    l = alpha * l_scr[...]
    acc = alpha * acc_scr[...]
    for sc, vc_ref in zip(scores, vc_refs):
        p = jnp.exp2(sc - m_new)
        l = l + jnp.sum(p, axis=-1, keepdims=True)
        acc = acc + _dot(p.astype(BF16), vc_ref[0, 0].astype(BF16))
    l_scr[...] = l
    acc_scr[...] = acc
    m_scr[...] = m_new

    @pl.when(s == n_steps - 1)
    def _():
        lam = _diff_lambda(lam_ref, lam_init)
        row = lax.broadcasted_iota(jnp.int32, (nq, ATT_BLK), 0) & (DEC_SEQ - 1)
        col = lax.broadcasted_iota(jnp.int32, (nq, ATT_BLK), 1)
        causal = col <= row
        pad = jnp.zeros((ATT_BLK - SROWS, LANES), F32)
        first_comp = lax.broadcasted_iota(jnp.int32, (nq, DV_DIFF), 0) < DEC_SEQ
        qf = qz.astype(F32)
        for h in range(n_heads):
            cols = slice(h * LANES, (h + 1) * LANES)
            rows = slice(h * nq, (h + 1) * nq)
            kblk = jnp.concatenate([kn_ref[:, cols], pad], axis=0).astype(BF16)
            vblk = jnp.concatenate([vn_ref[0, :, cols], pad], axis=0).astype(BF16)
            _, l, acc = _softmax_step(qf[rows].astype(BF16), kblk, vblk, m_scr[rows, :],
                                      l_scr[rows, :], acc_scr[rows, :], causal)
            o2 = acc / l
            o = o2 - lam * pltpu.roll(o2, DEC_SEQ, axis=0)
            y = o * lax.rsqrt(jnp.mean(o * o, axis=-1, keepdims=True) + EPS) * subg_ref[...]
            o_ref[:, cols] = jnp.where(first_comp, y * (1.0 - lam_init), 0.0)


def _diff_sample(qn_s, kn_s, proj_s, cache_k, cache_v, o, page_table, diff_lambda, sub_g, lam_init):
    sr, width = qn_s.shape
    nh = width // DV_DIFF
    nreq, n_pages = page_table.shape
    assert 2 * DEC_SEQ == SROWS
    pps = 4 if n_pages % 4 == 0 else 1
    n_steps = n_pages // pps
    nq = 2 * DEC_SEQ
    rows = nh * nq
    q4 = jnp.transpose(qn_s.reshape(nreq, SROWS, nh, DV_DIFF)[:, :DEC_SEQ], (0, 2, 1, 3))
    lane = jnp.arange(DV_DIFF)
    qz = jnp.stack([jnp.where(lane < HD_DIFF, q4, 0.0), jnp.where(lane >= HD_DIFF, q4, 0.0)], axis=2)
    qz = qz.reshape(nreq, rows, DV_DIFF).astype(BF16)
    same_head = (np.arange(PAGE_SIZE * nh)[None, :] % nh) == (np.arange(rows)[:, None] // nq)
    bias = jnp.asarray(np.where(same_head, 0.0, -np.inf), F32)
    body = functools.partial(_diff_sample_body, n_heads=nh, n_steps=n_steps, pages_per_step=pps,
                             lam_init=lam_init)
    page = lambda g: pl.BlockSpec((1, 1, PAGE_SIZE * nh, DV_DIFF),
                                  lambda b, s, pt: (o, pt[b, s * pps + g], 0, 0))
    return pl.pallas_call(
        body,
        grid_spec=pltpu.PrefetchScalarGridSpec(
            num_scalar_prefetch=1,
            grid=(nreq, n_steps),
            in_specs=[pl.BlockSpec((1, rows, DV_DIFF), lambda b, s, pt: (b, 0, 0)),
                      pl.BlockSpec((SROWS, width), lambda b, s, pt: (b, 0)),
                      pl.BlockSpec((1, SROWS, width), lambda b, s, pt: (2, b, 0)),
                      pl.BlockSpec(bias.shape, lambda b, s, pt: (0, 0))]
                     + [page(g) for g in range(pps)] + [page(g) for g in range(pps)]
                     + [pl.BlockSpec(diff_lambda.shape, lambda b, s, pt: (0, 0)),
                        pl.BlockSpec((1, DV_DIFF), lambda b, s, pt: (0, 0))],
            out_specs=pl.BlockSpec((SROWS, width), lambda b, s, pt: (b, 0)),
            scratch_shapes=[pltpu.VMEM((rows, 1), F32), pltpu.VMEM((rows, 1), F32),
                            pltpu.VMEM((rows, DV_DIFF), F32)],
        ),
        out_shape=jax.ShapeDtypeStruct((sr, width), F32),
        compiler_params=_cp(("arbitrary", "arbitrary")),
        name="diff_sample",
    )(page_table, qz, kn_s, proj_s, bias, *([cache_k] * pps), *([cache_v] * pps),
      diff_lambda, sub_g[None, :])


def _route(h, whi_ref, wlo_ref, b_ref, ltri_ref, cnt_scr, row_real):
    n = h.shape[0]
    hi, lo = _split_bf16(h)
    logits = _dot(hi, whi_ref[...]) + _dot(lo, whi_ref[...]) + _dot(hi, wlo_ref[...]) + b_ref[...]
    lane = lax.broadcasted_iota(jnp.int32, logits.shape, 1).astype(F32)
    neg = -jnp.inf
    first = lambda hit: jnp.min(jnp.where(hit, lane, float(LANES)), axis=-1, keepdims=True)
    gl = jnp.where(lane < N_GROUPS, logits, neg)
    gmax = jnp.max(gl, axis=-1, keepdims=True)
    g_top = 1.0 / jnp.sum(jnp.exp(gl - gmax), axis=-1, keepdims=True)
    g_idx = first(gl == gmax)
    lo_lane = N_GROUPS + EXP_PER_GROUP * g_idx
    el = jnp.where(jnp.logical_and(lane >= lo_lane, lane < lo_lane + EXP_PER_GROUP), logits, neg)
    emax = jnp.max(el, axis=-1, keepdims=True)
    esum = jnp.sum(jnp.exp(el - emax), axis=-1, keepdims=True)
    l1 = first(el == emax)
    el2 = jnp.where(lane == l1, neg, el)
    e2max = jnp.max(el2, axis=-1, keepdims=True)
    l2 = first(el2 == e2max)
    p1 = 1.0 / esum
    p2 = jnp.exp(e2max - emax) / esum
    w1 = p1 / (p1 + p2) * g_top
    w2 = p2 / (p1 + p2) * g_top
    e1 = l1 - N_GROUPS
    e2 = l2 - N_GROUPS
    hit1 = lane == e1
    hit2 = lane == e2
    both = jnp.logical_or(hit1, hit2)
    if row_real is not None:
        both = jnp.logical_and(both, row_real)
    both = jnp.where(both, 1.0, 0.0)
    before = _dot(ltri_ref[:n, :n], both.astype(BF16)) + cnt_scr[...]
    rank1 = jnp.sum(jnp.where(hit1, before, 0.0), axis=-1, keepdims=True)
    rank2 = jnp.sum(jnp.where(hit2, before, 0.0), axis=-1, keepdims=True)
    cnt_scr[...] += jnp.sum(both, axis=0, keepdims=True)
    rec = jnp.where(lane == 0, e1, jnp.where(lane == 1, e2, jnp.where(lane == 2, w1, jnp.where(
        lane == 3, w2, jnp.where(lane == 4, rank1, jnp.where(lane == 5, rank2, 0.0))))))
    return hi, rec


def _router_body(xp_ref, xs_ref, g_ref, sh_ref, sc_ref, shs_ref, scs_ref, whi_ref, wlo_ref, b_ref,
                 ltri_ref, h_ref, rec_ref, cnt_ref, cnt_scr, *, n_p, tiles_per_batch):
    i = pl.program_id(0)
    tm = xp_ref.shape[0]
    sr = xs_ref.shape[0]

    @pl.when(i == 0)
    def _():
        cnt_scr[...] = jnp.zeros_like(cnt_scr)

    @pl.when(i < n_p)
    def _():
        b = i // tiles_per_batch
        sh = sh_ref[0, 0, pl.ds(b, 1), :]
        sc = sc_ref[0, 0, pl.ds(b, 1), :]
        chunk = min(256, tm)

        def body(r, c):
            sl = pl.ds(pl.multiple_of(r * chunk, chunk), chunk)
            hi, rec = _route(_modulate(xp_ref[sl, :], g_ref[...], sh, sc), whi_ref, wlo_ref, b_ref,
                             ltri_ref, cnt_scr, None)
            h_ref[sl, :] = hi
            rec_ref[sl, :] = rec
            return c

        lax.fori_loop(0, tm // chunk, body, 0)

    @pl.when(i == n_p)
    def _():
        hi, rec = _route(_mod_sample_tile(xs_ref, g_ref[...], shs_ref, scs_ref), whi_ref, wlo_ref, b_ref,
                         ltri_ref, cnt_scr, _sample_row_valid(sr))
        h_ref[:sr, :] = hi
        rec_ref[:sr, :] = rec
        h_ref[sr:, :] = jnp.zeros((tm - sr, h_ref.shape[1]), h_ref.dtype)
        rec_ref[sr:, :] = jnp.zeros((tm - sr, LANES), F32)

    cnt_ref[...] = jnp.broadcast_to(cnt_scr[...], cnt_ref.shape)


def _router(xp, xs, g, mods, li, mod_s, w_rg, b_rg, w_re, b_re):
    tp, d = xp.shape
    sr = xs.shape[0]
    tm = _prompt_tile()
    n_p = tp // tm
    last = n_p - 1
    wr = jnp.concatenate([w_rg, jnp.moveaxis(w_re, 0, 1).reshape(d, N_EXPERTS),
                          jnp.zeros((d, LANES - N_GROUPS - N_EXPERTS), F32)], axis=1)
    whi = wr.astype(BF16)
    wlo = (wr - whi.astype(F32)).astype(BF16)
    bias = jnp.concatenate([b_rg, b_re.reshape(-1), jnp.zeros((LANES - N_GROUPS - N_EXPERTS,), F32)])[None]
    body = functools.partial(_router_body, n_p=n_p, tiles_per_batch=SEQ // tm)
    full = lambda r, c: pl.BlockSpec((r, c), lambda i: (0, 0))
    rows = tp + tm
    chunk = min(256, tm)
    ltri = jnp.asarray(np.tril(np.ones((chunk, chunk), np.float32), -1), dtype=BF16)
    return pl.pallas_call(
        body,
        grid=(n_p + 1,),
        in_specs=[pl.BlockSpec((tm, d), lambda i: (jnp.minimum(i, last), 0)),
                  full(sr, d),
                  pl.BlockSpec((1, d), lambda i: (0, 0)),
                  pl.BlockSpec((1, 1, MOD_ROWS, d), lambda i: (li, 3, 0, 0)),
                  pl.BlockSpec((1, 1, MOD_ROWS, d), lambda i: (li, 4, 0, 0)),
                  full(sr, d), full(sr, d), full(d, LANES), full(d, LANES), full(1, LANES),
                  full(chunk, chunk)],
        out_specs=[pl.BlockSpec((tm, d), lambda i: (i, 0)),
                   pl.BlockSpec((tm, LANES), lambda i: (i, 0)),
                   full(SUBLANES, LANES)],
        out_shape=[jax.ShapeDtypeStruct((rows, d), BF16),
                   jax.ShapeDtypeStruct((rows, LANES), F32),
                   jax.ShapeDtypeStruct((SUBLANES, LANES), F32)],
        scratch_shapes=[pltpu.VMEM((1, LANES), F32)],
        compiler_params=_cp(("arbitrary",)),
        name="router",
    )(xp, xs, g, mods, mods, mod_s[3], mod_s[4], whi, wlo, bias, ltri)


def _experts_body(te_ref, nv_ref, x_ref, wi_ref, wo_ref, y_ref, wi_scr, wo_scr):
    t = pl.program_id(0)
    f = wo_ref.shape[2]
    fresh = jnp.logical_or(t == 0, te_ref[t] != te_ref[jnp.maximum(t - 1, 0)])

    @pl.when(jnp.logical_and(t < nv_ref[0], fresh))
    def _():
        wi_scr[...] = wi_ref[0, 0].astype(BF16)
        wo_scr[...] = wo_ref[0, 0].astype(BF16)

    @pl.when(t < nv_ref[0])
    def _():
        hid = _dot(x_ref[...], wi_scr[...])
        gate = hid[:, f:]
        act = (gate * jax.nn.sigmoid(gate)) * hid[:, :f]
        y_ref[...] = _dot(act.astype(BF16), wo_scr[...]).astype(y_ref.dtype)

    @pl.when(t >= nv_ref[0])
    def _():
        y_ref[...] = jnp.zeros_like(y_ref)


def _experts(xs_sorted, tile_expert, n_valid, w_in, w_out, li):
    p, d = xs_sorted.shape
    f2 = w_in.shape[3]
    return pl.pallas_call(
        _experts_body,
        grid_spec=pltpu.PrefetchScalarGridSpec(
            num_scalar_prefetch=2,
            grid=(p // EXP_TILE,),
            in_specs=[pl.BlockSpec((EXP_TILE, d), lambda t, te, nv: (t, 0)),
                      pl.BlockSpec((1, 1, d, f2), lambda t, te, nv: (li, te[t], 0, 0)),
                      pl.BlockSpec((1, 1, f2 // 2, d), lambda t, te, nv: (li, te[t], 0, 0))],
            out_specs=pl.BlockSpec((EXP_TILE, d), lambda t, te, nv: (t, 0)),
            scratch_shapes=[pltpu.VMEM((d, f2), BF16), pltpu.VMEM((f2 // 2, d), BF16)],
        ),
        out_shape=jax.ShapeDtypeStruct((p, d), BF16),
        compiler_params=_cp(("arbitrary",)),
        name="experts",
    )(tile_expert, n_valid, xs_sorted, w_in, w_out)


def _combine_body(y1p_ref, y2p_ref, y1s_ref, y2s_ref, recp_ref, recs_ref, xp_ref, xs_ref,
                  gate_ref, gates_ref, op_ref, os_ref, *, n_p, tiles_per_batch):
    i = pl.program_id(0)

    def mix(y1, y2, rec):
        return rec[:, 2:3] * y1.astype(F32) + rec[:, 3:4] * y2.astype(F32)

    @pl.when(i < n_p)
    def _():
        b = i // tiles_per_batch
        gate = gate_ref[0, 0, pl.ds(b, 1), :]
        op_ref[...] = xp_ref[...] + gate * mix(y1p_ref[0], y2p_ref[0], recp_ref[...])

    @pl.when(i == n_p)
    def _():
        os_ref[...] = xs_ref[...] + gates_ref[...] * mix(y1s_ref[0], y2s_ref[0], recs_ref[...])


def _combine(yg, rec, xp, xs, mods, li, mod_s):
    tp, d = xp.shape
    sr = xs.shape[0]
    tm = min(512, SEQ)
    n_p = tp // tm
    last = n_p - 1
    stile = tp // sr
    body = functools.partial(_combine_body, n_p=n_p, tiles_per_batch=SEQ // tm)
    pmap = lambda i: (jnp.minimum(i, last), 0)
    return pl.pallas_call(
        body,
        grid=(n_p + 1,),
        in_specs=[pl.BlockSpec((1, tm, d), lambda i: (0, jnp.minimum(i, last), 0)),
                  pl.BlockSpec((1, tm, d), lambda i: (1, jnp.minimum(i, last), 0)),
                  pl.BlockSpec((1, sr, d), lambda i: (0, stile, 0)),
                  pl.BlockSpec((1, sr, d), lambda i: (1, stile, 0)),
                  pl.BlockSpec((tm, LANES), pmap),
                  pl.BlockSpec((sr, LANES), lambda i: (stile, 0)),
                  pl.BlockSpec((tm, d), pmap),
                  pl.BlockSpec((sr, d), lambda i: (0, 0)),
                  pl.BlockSpec((1, 1, MOD_ROWS, d), lambda i: (li, 5, 0, 0)),
                  pl.BlockSpec((sr, d), lambda i: (0, 0))],
        out_specs=[pl.BlockSpec((tm, d), pmap), pl.BlockSpec((sr, d), lambda i: (0, 0))],
        out_shape=[jax.ShapeDtypeStruct((tp, d), F32), jax.ShapeDtypeStruct((sr, d), F32)],
        compiler_params=_cp(("arbitrary",)),
        name="combine",
    )(yg, yg, yg, yg, rec, rec, xp, xs, mods, mod_s[5])


def _moe(xp, xs, g, mods, li, mod_s, w_rg, b_rg, w_re, b_re, w_in, w_out):
    tp, d = xp.shape
    sr = xs.shape[0]
    h_all, rec, counts = _router(xp, xs, g, mods, li, mod_s, w_rg, b_rg, w_re, b_re)
    nt = tp + sr
    zero_row = nt
    row = jnp.arange(nt)
    real = jnp.logical_or(row < tp, (row - tp) % SROWS < DEC_SEQ)
    ids = rec[:nt, :6].astype(jnp.int32)
    eid, rank = ids[:, 0:2], ids[:, 4:6]
    count = counts[0, :N_EXPERTS].astype(jnp.int32)
    padded = (count + EXP_TILE - 1) // EXP_TILE * EXP_TILE
    ends = jnp.cumsum(padded)
    start = ends - padded
    n_real = 2 * (tp + (sr // SROWS) * DEC_SEQ)
    p = (n_real + N_EXPERTS * (EXP_TILE - 1) + EXP_TILE - 1) // EXP_TILE * EXP_TILE
    pos = jnp.where(real[:, None], start[eid] + rank, p).T.reshape(-1)
    src = jnp.full((p,), zero_row, jnp.int32).at[pos].set(jnp.tile(row, 2), mode="drop")
    tile_start = jnp.arange(p // EXP_TILE) * EXP_TILE
    tile_expert = jnp.minimum(jnp.sum((ends[None, :] <= tile_start[:, None]).astype(jnp.int32), axis=1),
                              N_EXPERTS - 1)
    n_valid = (ends[-1] // EXP_TILE).astype(jnp.int32)[None]
    xs_sorted = jnp.take(h_all, src, axis=0)
    ys = _experts(xs_sorted, tile_expert, n_valid, w_in, w_out, li)
    yg = jnp.take(ys, jnp.minimum(pos, p - 1), axis=0).reshape(2, nt, d)
    return _combine(yg, rec, xp, xs, mods, li, mod_s)


def kernel(x_prompt, x_sample, cache_sb_k, cache_sb_v, state_ret, cache_diff_k, cache_diff_v, page_table, c_prompt, c_sample, w_ada, b_ada, norm_mix, norm_ffn, w_in_even, w_out_even, ret_gn_gain, w_in_odd, w_out_odd, qk_gain, diff_lambda, diff_subln, w_router_group, b_router_group, w_router_expert, b_router_expert, w_expert_in, w_expert_out):
    nb, seq, d = x_prompt.shape
    nreq, dec_seq, _ = x_sample.shape
    n_pages = page_table.shape[1]
    past_len = n_pages * PAGE_SIZE
    tp = nb * seq
    sr = nreq * SROWS
    h_sb = cache_sb_k.shape[3]
    h_diff = cache_diff_k.shape[3]

    xp = x_prompt.reshape(tp, d)
    xs = jnp.pad(x_sample, ((0, 0), (0, SROWS - dec_seq), (0, 0))).reshape(sr, d)
    c_all = jnp.concatenate([c_prompt, c_sample, jnp.zeros((MOD_ROWS - nb - nreq, d), F32)], axis=0)
    mods = _ada(c_all, w_ada, b_ada)
    uu = _cumsum_mat(ATT_BLK)
    csk = cache_sb_k.reshape(cache_sb_k.shape[0], cache_sb_k.shape[1], PAGE_SIZE * h_sb, HD_SB)
    csv = cache_sb_v.reshape(csk.shape)
    cdk = cache_diff_k.reshape(cache_diff_k.shape[0], cache_diff_k.shape[1], PAGE_SIZE * h_diff, DV_DIFF)
    cdv = cache_diff_v.reshape(cdk.shape)
    pos_s = past_len + (jnp.arange(sr) % SROWS)

    def sample_rows(a, tail):
        return a.reshape(nreq, SROWS, *tail)[:, :dec_seq]

    sbk_p, sbv_p, sbk_s, sbv_s, ret_p, ret_s = [], [], [], [], [], []
    dk_p, dv_p, dk_s, dv_s = [], [], [], []
    for li in range(DEPTH):
        mod_s = [jnp.repeat(mods[li, k, nb:nb + nreq], SROWS, axis=0) for k in range(ADA_MOD)]
        if li % 2 == 0:
            e = li // 2
            w_in = w_in_even[e].astype(BF16)
            proj_p, proj_s = _inproj(xp, xs, norm_mix[li][None], mods, li, 0, 1, mod_s, w_in, h_sb * HD_SB)
            gain = ret_gn_gain[e][None, :]
            o_sb_p = _sb_prompt(proj_p, uu)
            o_r_p, s_p = _ret_prompt(proj_p, gain)
            o_sb_s = _sb_sample(proj_s, csk, csv, e, page_table, uu)
            o_r_s, s_s = _ret_sample(proj_s, gain, state_ret[e], past_len)
            xp, xs = _outproj((o_sb_p, o_r_p), (o_sb_s, o_r_s), w_out_even[e].astype(BF16),
                              xp, xs, mods, li, 2, mod_s)
            sbk_p.append(proj_p[1].reshape(nb, seq, h_sb, HD_SB))
            sbv_p.append(proj_p[2].reshape(nb, seq, h_sb, HD_SB))
            sbk_s.append(sample_rows(proj_s[1], (h_sb, HD_SB)))
            sbv_s.append(sample_rows(proj_s[2], (h_sb, HD_SB)))
            ret_p.append(s_p)
            ret_s.append(s_s)
        else:
            o = li // 2
            lam_init = 0.8 - 0.6 * math.exp(-0.3 * li)
            w_in = w_in_odd[o].astype(BF16)
            proj_p, proj_s = _inproj(xp, xs, norm_mix[li][None], mods, li, 0, 1, mod_s, w_in, h_diff * DV_DIFF)
            qn_p, kn_p = _diff_prep(proj_p, qk_gain[o], jnp.arange(seq), seq, BF16)
            qn_s, kn_s = _diff_prep(proj_s, qk_gain[o], pos_s, sr, F32)
            o_p = _diff_prompt(qn_p, kn_p, proj_p, diff_lambda[o], diff_subln[o], lam_init)
            o_s = _diff_sample(qn_s, kn_s, proj_s, cdk, cdv, o, page_table, diff_lambda[o],
                               diff_subln[o], lam_init)
            xp, xs = _outproj(o_p, o_s, w_out_odd[o].astype(BF16), xp, xs, mods, li, 2, mod_s)
            dk_p.append(kn_p.reshape(nb, seq, h_diff, DV_DIFF))
            dv_p.append(proj_p[2].reshape(nb, seq, h_diff, DV_DIFF))
            dk_s.append(sample_rows(kn_s, (h_diff, DV_DIFF)))
            dv_s.append(sample_rows(proj_s[2], (h_diff, DV_DIFF)))
        xp, xs = _moe(xp, xs, norm_ffn[li][None], mods, li, mod_s, w_router_group[li], b_router_group[li],
                      w_router_expert[li], b_router_expert[li], w_expert_in, w_expert_out)
    return (xp.reshape(nb, seq, d), sample_rows(xs, (d,)),
            jnp.stack(sbk_p), jnp.stack(sbv_p), jnp.stack(sbk_s), jnp.stack(sbv_s),
            jnp.stack(ret_p), jnp.stack(ret_s), jnp.stack(dk_p), jnp.stack(dv_p),
            jnp.stack(dk_s), jnp.stack(dv_s))
```

```python
import functools
import math

import jax
import jax.numpy as jnp
import numpy as np
from jax import lax
from jax.experimental import pallas as pl
from jax.experimental.pallas import tpu as pltpu

D_MODEL = 2048
BATCH = 4
SEQ = 2048
DEPTH = 4
DEC_BATCH = 8
DEC_SEQ = 4
PAGE_SIZE = 128
EPS = 1e-6
HD_SB = 128
DK_RET = 128
DV_RET = 256
RET_THETA = 10000.0
HD_DIFF = 64
DV_DIFF = 2 * HD_DIFF
ROPE_THETA = 500000.0
ROT_DIM = HD_DIFF // 4
N_GROUPS = 4
EXP_PER_GROUP = 4
N_EXPERTS = N_GROUPS * EXP_PER_GROUP
D_EXPERT = 512
ADA_MOD = 6

F32 = jnp.float32
BF16 = jnp.bfloat16

LANES = 128
SUBLANES = 8
MOD_ROWS = 16
SROWS = SUBLANES
ATT_BLK = 128
EXP_TILE = 256
SB_EXIT = -104.0
VMEM_LIMIT = 56 * 2**20


def _cp(sem, vmem=VMEM_LIMIT):
    return pltpu.CompilerParams(dimension_semantics=sem, vmem_limit_bytes=vmem)


def _dot(a, b):
    return jnp.dot(a, b, preferred_element_type=F32)


def _dot_nt(a, b):
    return lax.dot_general(a, b, (((1,), (1,)), ((), ())), preferred_element_type=F32)


def _split_bf16(x):
    hi = x.astype(BF16)
    lo = (x - hi.astype(F32)).astype(BF16)
    return hi, lo


def _modulate(x, g, shift, scale):
    y = x * lax.rsqrt(jnp.mean(x * x, axis=-1, keepdims=True) + EPS) * g
    return y * (1.0 + scale) + shift


def _sample_row_valid(rows):
    r = lax.broadcasted_iota(jnp.int32, (rows, 1), 0)
    return (r & (SROWS - 1)) < DEC_SEQ


def _ada_body(c_ref, w_ref, b_ref, o_ref):
    c = c_ref[...]
    a = (c * jax.nn.sigmoid(c)).astype(BF16)
    o_ref[0, 0] = _dot(a, w_ref[0].astype(BF16)) + b_ref[0]


def _ada(c_all, w_ada, b_ada):
    depth, d, _ = w_ada.shape
    tn = min(1024, d)
    nj = d // tn
    return pl.pallas_call(
        _ada_body,
        grid=(depth, ADA_MOD, nj),
        in_specs=[
            pl.BlockSpec((MOD_ROWS, d), lambda l, k, j: (0, 0)),
            pl.BlockSpec((1, d, tn), lambda l, k, j: (l, 0, k * nj + j)),
            pl.BlockSpec((1, 1, tn), lambda l, k, j: (l, 0, k * nj + j)),
        ],
        out_specs=pl.BlockSpec((1, 1, MOD_ROWS, tn), lambda l, k, j: (l, k, 0, j)),
        out_shape=jax.ShapeDtypeStruct((depth, ADA_MOD, MOD_ROWS, d), F32),
        compiler_params=_cp(("arbitrary",) * 3),
        name="ada",
    )(c_all, w_ada, b_ada.reshape(depth, 1, -1))


def _prompt_tile():
    return min(1024, SEQ)


def _mod_prompt_tile(x_ref, h_ref, g, sh, sc, rows):
    chunk = min(256, rows)

    def body(r, c):
        sl = pl.ds(pl.multiple_of(r * chunk, chunk), chunk)
        h_ref[sl, :] = _modulate(x_ref[sl, :], g, sh, sc).astype(h_ref.dtype)
        return c

    lax.fori_loop(0, rows // chunk, body, 0)


def _mod_sample_tile(xs_ref, g, shs_ref, scs_ref):
    h = _modulate(xs_ref[...], g, shs_ref[...], scs_ref[...])
    return jnp.where(_sample_row_valid(xs_ref.shape[0]), h, 0.0)


def _inproj_body(xp_ref, xs_ref, g_ref, sh_ref, sc_ref, shs_ref, scs_ref, w_ref,
                 op_ref, os_ref, hp_scr, hs_scr, *, n_p, tiles_per_batch):
    i = pl.program_id(0)
    j = pl.program_id(1)
    tm = xp_ref.shape[0]

    @pl.when(jnp.logical_and(i < n_p, j == 0))
    def _():
        b = i // tiles_per_batch
        _mod_prompt_tile(xp_ref, hp_scr, g_ref[...], sh_ref[0, 0, pl.ds(b, 1), :],
                         sc_ref[0, 0, pl.ds(b, 1), :], tm)

    @pl.when(jnp.logical_and(i == n_p, j == 0))
    def _():
        hs_scr[...] = _mod_sample_tile(xs_ref, g_ref[...], shs_ref, scs_ref).astype(BF16)

    @pl.when(i < n_p)
    def _():
        chunk = min(512, tm)

        def body(r, c):
            sl = pl.ds(pl.multiple_of(r * chunk, chunk), chunk)
            op_ref[0, sl, :] = _dot(hp_scr[sl, :], w_ref[...])
            return c

        lax.fori_loop(0, tm // chunk, body, 0)

    @pl.when(i == n_p)
    def _():
        os_ref[0] = _dot(hs_scr[...], w_ref[...])


def _inproj(xp, xs, g, mods, li, k_shift, k_scale, mod_s, w, group_w):
    tp, d = xp.shape
    sr = xs.shape[0]
    n = w.shape[1]
    tm = _prompt_tile()
    n_p = tp // tm
    tn = min(1024, group_w)
    nj = n // tn
    tpg = group_w // tn
    last = n_p - 1

    def p_out(i, j):
        jj = jnp.where(i == n_p, nj - 1, j)
        return (jj // tpg, jnp.minimum(i, last), jj % tpg)

    def s_out(i, j):
        jj = jnp.where(i == n_p, j, 0)
        return (jj // tpg, 0, jj % tpg)

    body = functools.partial(_inproj_body, n_p=n_p, tiles_per_batch=SEQ // tm)
    return pl.pallas_call(
        body,
        grid=(n_p + 1, nj),
        in_specs=[
            pl.BlockSpec((tm, d), lambda i, j: (jnp.minimum(i, last), 0)),
            pl.BlockSpec((sr, d), lambda i, j: (0, 0)),
            pl.BlockSpec((1, d), lambda i, j: (0, 0)),
            pl.BlockSpec((1, 1, MOD_ROWS, d), lambda i, j: (li, k_shift, 0, 0)),
            pl.BlockSpec((1, 1, MOD_ROWS, d), lambda i, j: (li, k_scale, 0, 0)),
            pl.BlockSpec((sr, d), lambda i, j: (0, 0)),
            pl.BlockSpec((sr, d), lambda i, j: (0, 0)),
            pl.BlockSpec((d, tn), lambda i, j: (0, j)),
        ],
        out_specs=[
            pl.BlockSpec((1, tm, tn), p_out),
            pl.BlockSpec((1, sr, tn), s_out),
        ],
        out_shape=[
            jax.ShapeDtypeStruct((n // group_w, tp, group_w), F32),
            jax.ShapeDtypeStruct((n // group_w, sr, group_w), F32),
        ],
        scratch_shapes=[pltpu.VMEM((tm, d), BF16), pltpu.VMEM((sr, d), BF16)],
        compiler_params=_cp(("arbitrary", "arbitrary")),
        name="inproj",
    )(xp, xs, g, mods, mods, mod_s[k_shift], mod_s[k_scale], w)


def _outproj_body(a1p_ref, a2p_ref, a1s_ref, a2s_ref, w1_ref, w2_ref, xp_ref, xs_ref,
                  gate_ref, gates_ref, op_ref, os_ref, *, n_p, tiles_per_batch):
    i = pl.program_id(0)
    tm = xp_ref.shape[0]

    @pl.when(i < n_p)
    def _():
        b = i // tiles_per_batch
        gate = gate_ref[0, 0, pl.ds(b, 1), :]
        chunk = min(512, tm)

        def body(r, c):
            sl = pl.ds(pl.multiple_of(r * chunk, chunk), chunk)
            y = _dot(a1p_ref[sl, :], w1_ref[...]) + _dot(a2p_ref[sl, :], w2_ref[...])
            op_ref[sl, :] = xp_ref[sl, :] + gate * y
            return c

        lax.fori_loop(0, tm // chunk, body, 0)

    @pl.when(i == n_p)
    def _():
        y = (_dot(a1s_ref[...].astype(BF16), w1_ref[...])
             + _dot(a2s_ref[...].astype(BF16), w2_ref[...]))
        os_ref[...] = xs_ref[...] + gates_ref[...] * y


def _outproj(ap, as_, w, xp, xs, mods, li, k_gate, mod_s):
    tp, d = xp.shape
    sr = xs.shape[0]
    tm = _prompt_tile()
    n_p = tp // tm
    tn = min(1024, d)
    nj = d // tn
    last = n_p - 1
    half = w.shape[0] // 2
    if isinstance(ap, tuple):
        a1p, a2p, a1s, a2s = ap[0], ap[1], as_[0], as_[1]
        c2 = 0
    else:
        a1p = a2p = ap
        a1s = a2s = as_
        c2 = 1

    def pin(c):
        return lambda i, j: (jnp.minimum(i, last), c)

    def p_out(i, j):
        return (jnp.minimum(i, last), jnp.where(i == n_p, nj - 1, j))

    def s_out(i, j):
        return (0, jnp.where(i == n_p, j, 0))

    body = functools.partial(_outproj_body, n_p=n_p, tiles_per_batch=SEQ // tm)
    return pl.pallas_call(
        body,
        grid=(n_p + 1, nj),
        in_specs=[
            pl.BlockSpec((tm, half), pin(0)),
            pl.BlockSpec((tm, half), pin(c2)),
            pl.BlockSpec((sr, half), lambda i, j: (0, 0)),
            pl.BlockSpec((sr, half), lambda i, j: (0, c2)),
            pl.BlockSpec((half, tn), lambda i, j: (0, j)),
            pl.BlockSpec((half, tn), lambda i, j: (1, j)),
            pl.BlockSpec((tm, tn), lambda i, j: (jnp.minimum(i, last), j)),
            pl.BlockSpec((sr, tn), lambda i, j: (0, j)),
            pl.BlockSpec((1, 1, MOD_ROWS, tn), lambda i, j: (li, k_gate, 0, j)),
            pl.BlockSpec((sr, tn), lambda i, j: (0, j)),
        ],
        out_specs=[pl.BlockSpec((tm, tn), p_out), pl.BlockSpec((sr, tn), s_out)],
        out_shape=[jax.ShapeDtypeStruct((tp, d), F32), jax.ShapeDtypeStruct((sr, d), F32)],
        compiler_params=_cp(("arbitrary", "arbitrary")),
        name="outproj",
    )(a1p, a2p, a1s, a2s, w, w, xp, xs, mods, mod_s[k_gate])


def _cumsum_mat(n):
    j = np.arange(n)[:, None]
    s = np.arange(n)[None, :]
    blk = np.concatenate([(j >= s).astype(np.float32), np.ones((n, n), np.float32)], axis=1)
    return jnp.asarray(np.concatenate([blk, blk], axis=0), dtype=BF16)


def _sb_block(qb, kblk, vblk, uu, carry, mask, scale):
    n = kblk.shape[0]
    z = _dot_nt(qb, kblk) * scale
    lk = -(jnp.maximum(z, 0.0) + jnp.log1p(jnp.exp(-jnp.abs(z))))
    if mask is not None:
        lk = jnp.where(mask, lk, 0.0)
    hi, lo = _split_bf16(lk)
    incl = [None] * (n // ATT_BLK)
    for g in reversed(range(n // ATT_BLK)):
        cols = slice(g * ATT_BLK, (g + 1) * ATT_BLK)
        r = _dot(jnp.concatenate([hi[:, cols], lo[:, cols]], axis=1), uu)
        incl[g] = r[:, :ATT_BLK] + carry
        carry = carry + r[:, ATT_BLK:]
    a = jnp.exp(z + jnp.concatenate(incl, axis=1))
    if mask is not None:
        a = jnp.where(mask, a, 0.0)
    return carry, _dot(a.astype(BF16), vblk)


def _sb_prompt_body(q_ref, k_ref, v_ref, uu_ref, o_ref, kb_scr, vb_scr, carry_scr, acc_scr,
                    *, nq, tile, n_heads, scale):
    kb_scr[...] = k_ref[0].astype(BF16)
    vb_scr[...] = v_ref[0].astype(BF16)
    row = lax.broadcasted_iota(jnp.int32, (tile, tile), 0)
    col = lax.broadcasted_iota(jnp.int32, (tile, tile), 1)
    strict = col < row
    uu = uu_ref[...]
    head_cols = [slice(h * HD_SB, (h + 1) * HD_SB) for h in range(n_heads)]

    def q_body(qi, c):
        rows = pl.ds(pl.multiple_of(qi * tile, tile), tile)

        def blocks(rk, first):
            mx = None
            for h, cols in enumerate(head_cols):
                qb = q_ref[0, rows, cols].astype(BF16)
                carry0 = jnp.zeros((tile, ATT_BLK), F32) if first else carry_scr[h]
                carry, o = _sb_block(qb, kb_scr[rk, cols], vb_scr[rk, cols], uu, carry0,
                                     strict if first else None, scale)
                carry_scr[h] = carry
                acc_scr[h] = o if first else acc_scr[h] + o
                top = jnp.max(carry)
                mx = top if mx is None else jnp.maximum(mx, top)
            return mx

        def cond(st):
            kb, mx = st
            return jnp.logical_and(kb >= 0, mx > SB_EXIT)

        def body(st):
            kb, _ = st
            return kb - 1, blocks(pl.ds(pl.multiple_of(kb * tile, tile), tile), False)

        lax.while_loop(cond, body, (qi - 1, blocks(rows, True)))
        for h, cols in enumerate(head_cols):
            o_ref[rows, cols] = acc_scr[h].astype(o_ref.dtype)
        return c

    lax.fori_loop(0, nq, q_body, 0)


def _sb_prompt(proj, uu):
    _, tp, width = proj.shape
    nb = tp // SEQ
    nh = width // HD_SB
    hpb = min(2, nh)
    tile = min(256, SEQ)
    w = hpb * HD_SB
    body = functools.partial(_sb_prompt_body, nq=SEQ // tile, tile=tile, n_heads=hpb,
                             scale=HD_SB ** -0.5)
    spec = lambda g: pl.BlockSpec((1, SEQ, w), lambda b, h: (g, b, h))
    return pl.pallas_call(
        body,
        grid=(nb, nh // hpb),
        in_specs=[spec(0), spec(1), spec(2),
                  pl.BlockSpec((2 * ATT_BLK, 2 * ATT_BLK), lambda b, h: (0, 0))],
        out_specs=pl.BlockSpec((SEQ, w), lambda b, h: (b, h)),
        out_shape=jax.ShapeDtypeStruct((tp, width), BF16),
        scratch_shapes=[pltpu.VMEM((SEQ, w), BF16), pltpu.VMEM((SEQ, w), BF16),
                        pltpu.VMEM((hpb, tile, ATT_BLK), F32), pltpu.VMEM((hpb, tile, HD_SB), F32)],
        compiler_params=_cp(("arbitrary", "arbitrary")),
        name="sb_prompt",
    )(proj, proj, proj, uu)


def _sb_sample_body(pt_ref, q_ref, kn_ref, vn_ref, kc_hbm, vc_hbm, uu_ref, o_ref,
                    kbuf, vbuf, sem, carry_scr, acc_scr, *, layer, n_heads, n_pages, scale):
    b = pl.program_id(0)
    blk = ATT_BLK
    uu = uu_ref[...]
    row = lax.broadcasted_iota(jnp.int32, (SROWS, blk), 0)
    col = lax.broadcasted_iota(jnp.int32, (SROWS, blk), 1)
    real_row = row < DEC_SEQ
    strict = col < row
    pad = jnp.zeros((blk - SROWS, HD_SB), F32)

    def blocks(first):
        mx = None
        for h in range(n_heads):
            cols = slice(h * HD_SB, (h + 1) * HD_SB)
            rows = slice(h * SROWS, (h + 1) * SROWS)
            qb = q_ref[0, :, cols].astype(BF16)
            if first:
                kblk = jnp.concatenate([kn_ref[0, :, cols], pad], axis=0).astype(BF16)
                vblk = jnp.concatenate([vn_ref[0, :, cols], pad], axis=0).astype(BF16)
                carry0 = jnp.zeros((SROWS, blk), F32)
            else:
                kblk = kbuf[pl.ds(h, PAGE_SIZE, stride=n_heads), :].astype(BF16)
                vblk = vbuf[pl.ds(h, PAGE_SIZE, stride=n_heads), :].astype(BF16)
                carry0 = carry_scr[rows, :]
            carry, o = _sb_block(qb, kblk, vblk, uu, carry0, strict if first else None, scale)
            carry_scr[rows, :] = carry
            acc_scr[rows, :] = o if first else acc_scr[rows, :] + o
            top = jnp.max(jnp.where(real_row, carry, -jnp.inf))
            mx = top if mx is None else jnp.maximum(mx, top)
        return mx

    def cond(st):
        p, mx = st
        return jnp.logical_and(p >= 0, mx > SB_EXIT)

    def body(st):
        p, _ = st
        page = pt_ref[b, p]
        ck = pltpu.make_async_copy(kc_hbm.at[layer, page], kbuf, sem.at[0])
        cv = pltpu.make_async_copy(vc_hbm.at[layer, page], vbuf, sem.at[1])
        ck.start()
        cv.start()
        ck.wait()
        cv.wait()
        return p - 1, blocks(False)

    lax.while_loop(cond, body, (n_pages - 1, blocks(True)))
    for h in range(n_heads):
        o_ref[:, h * HD_SB:(h + 1) * HD_SB] = acc_scr[h * SROWS:(h + 1) * SROWS, :]


def _sb_sample(proj_s, cache_k, cache_v, e, page_table, uu):
    _, sr, width = proj_s.shape
    nh = width // HD_SB
    nreq, n_pages = page_table.shape
    body = functools.partial(_sb_sample_body, layer=e, n_heads=nh, n_pages=n_pages,
                             scale=HD_SB ** -0.5)
    spec = lambda g: pl.BlockSpec((1, SROWS, width), lambda b, pt: (g, b, 0))
    return pl.pallas_call(
        body,
        grid_spec=pltpu.PrefetchScalarGridSpec(
            num_scalar_prefetch=1,
            grid=(nreq,),
            in_specs=[spec(0), spec(1), spec(2),
                      pl.BlockSpec(memory_space=pl.ANY),
                      pl.BlockSpec(memory_space=pl.ANY),
                      pl.BlockSpec((2 * ATT_BLK, 2 * ATT_BLK), lambda b, pt: (0, 0))],
            out_specs=pl.BlockSpec((SROWS, width), lambda b, pt: (b, 0)),
            scratch_shapes=[pltpu.VMEM((PAGE_SIZE * nh, HD_SB), F32),
                            pltpu.VMEM((PAGE_SIZE * nh, HD_SB), F32),
                            pltpu.SemaphoreType.DMA((2,)),
                            pltpu.VMEM((nh * SROWS, ATT_BLK), F32),
                            pltpu.VMEM((nh * SROWS, HD_SB), F32)],
        ),
        out_shape=jax.ShapeDtypeStruct((sr, width), F32),
        compiler_params=_cp(("arbitrary",)),
        name="sb_sample",
    )(page_table, proj_s, proj_s, proj_s, cache_k, cache_v, uu)


def _ret_tables(pos, chunk, n_heads):
    half = DK_RET // 2
    inv = np.float32(RET_THETA) ** (-np.arange(half, dtype=np.float32) / half)
    ang = pos.astype(F32)[:, None] * jnp.asarray(inv)[None, :]
    cos = jnp.concatenate([jnp.cos(ang), jnp.cos(ang)], axis=1)
    sin = jnp.concatenate([-jnp.sin(ang), jnp.sin(ang)], axis=1)
    log_g = jnp.log1p(-(2.0 ** (-5.0 - jnp.arange(n_heads, dtype=F32))))
    i = jnp.arange(chunk, dtype=F32)
    return cos, sin, log_g, i


def _rope_half(x, cos, sin):
    return x * cos + pltpu.roll(x, x.shape[-1] // 2, axis=1) * sin


def _head_ln_gate(o, gain, g):
    mu = jnp.mean(o, axis=-1, keepdims=True)
    d = o - mu
    var = jnp.mean(d * d, axis=-1, keepdims=True)
    return d * lax.rsqrt(var + EPS) * gain * (g * jax.nn.sigmoid(g))


def _ret_prompt_body(q_ref, k_ref, v_ref, g_ref, cos_ref, sin_ref, dec_ref, qd_ref, kd_ref,
                     gc_ref, gain_ref, o_ref, s_ref, s_scr, *, n_chunks):
    c = ATT_BLK
    s_scr[...] = jnp.zeros_like(s_scr)
    decay = dec_ref[0]
    qd = qd_ref[0]
    kd = kd_ref[0]
    gc = gc_ref[0, 0:1, :1]
    gain = gain_ref[...]

    def body(n, carry):
        rows = pl.ds(pl.multiple_of(n * c, c), c)
        cos = cos_ref[rows, :]
        sin = sin_ref[rows, :]
        q = _rope_half(q_ref[0, rows, :], cos, sin)
        k = _rope_half(k_ref[0, rows, :], cos, sin) * (DK_RET ** -0.5)
        vb = v_ref[0, rows, :].astype(BF16)
        s_prev = s_scr[...]
        scores = _dot_nt(q.astype(BF16), k.astype(BF16)) * decay
        o = _dot(scores.astype(BF16), vb) + _dot((q * qd).astype(BF16), s_prev.astype(BF16))
        kv = _dot(jnp.transpose(k * kd).astype(BF16), vb)
        s_scr[...] = gc * s_prev + kv
        o_ref[rows, :] = _head_ln_gate(o, gain, g_ref[0, rows, :]).astype(o_ref.dtype)
        return carry

    lax.fori_loop(0, n_chunks, body, 0)
    s_ref[0, 0] = s_scr[...]


def _ret_prompt(proj, gain):
    _, tp, width = proj.shape
    nb = tp // SEQ
    nh = width // DV_RET
    c = ATT_BLK
    cos, sin, log_g, i = _ret_tables(jnp.arange(SEQ), c, nh)
    diff = i[:, None] - i[None, :]
    decay = jnp.where(diff >= 0, jnp.exp(jnp.maximum(diff, 0.0)[None] * log_g[:, None, None]), 0.0)
    ones = jnp.ones((1, 1, DK_RET), F32)
    qd = jnp.exp((i + 1)[None, :, None] * log_g[:, None, None]) * ones
    kd = jnp.exp((c - 1 - i)[None, :, None] * log_g[:, None, None]) * ones
    gc = jnp.exp(c * log_g)[:, None, None] * jnp.ones((1, SUBLANES, LANES), F32)
    body = functools.partial(_ret_prompt_body, n_chunks=SEQ // c)
    tab = lambda: pl.BlockSpec((SEQ, DK_RET), lambda b, h: (0, 0))
    per_head = lambda r, w: pl.BlockSpec((1, r, w), lambda b, h: (h, 0, 0))
    return pl.pallas_call(
        body,
        grid=(nb, nh),
        in_specs=[
            pl.BlockSpec((1, SEQ, DK_RET), lambda b, h: (3, b, h)),
            pl.BlockSpec((1, SEQ, DK_RET), lambda b, h: (3, b, nh + h)),
            pl.BlockSpec((1, SEQ, DV_RET), lambda b, h: (4, b, h)),
            pl.BlockSpec((1, SEQ, DV_RET), lambda b, h: (5, b, h)),
            tab(), tab(),
            per_head(c, c), per_head(c, DK_RET), per_head(c, DK_RET), per_head(SUBLANES, LANES),
            pl.BlockSpec((1, DV_RET), lambda b, h: (0, h)),
        ],
        out_specs=[pl.BlockSpec((SEQ, DV_RET), lambda b, h: (b, h)),
                   pl.BlockSpec((1, 1, DK_RET, DV_RET), lambda b, h: (b, h, 0, 0))],
        out_shape=[jax.ShapeDtypeStruct((tp, width), BF16),
                   jax.ShapeDtypeStruct((nb, nh, DK_RET, DV_RET), F32)],
        scratch_shapes=[pltpu.VMEM((DK_RET, DV_RET), F32)],
        compiler_params=_cp(("arbitrary", "arbitrary")),
        name="ret_prompt",
    )(proj, proj, proj, proj, cos, sin, decay, qd, kd, gc, gain)


def _ret_sample_body(q_ref, k_ref, v_ref, g_ref, cos_ref, sin_ref, dec_ref, qd_ref, kd_ref,
                     gc_ref, gain_ref, s0_ref, o_ref, s_ref, *, n_req):
    sr = q_ref.shape[1]
    cos = cos_ref[...]
    sin = sin_ref[...]
    q = _rope_half(q_ref[0], cos, sin)
    k = _rope_half(k_ref[0], cos, sin) * (DK_RET ** -0.5)
    v = v_ref[0]
    vb = v.astype(BF16)
    scores = _dot_nt(q.astype(BF16), k.astype(BF16)) * dec_ref[0]
    o = _dot(scores.astype(BF16), vb)
    qdec = q * qd_ref[0]
    kdec = k * kd_ref[0]
    gc = gc_ref[0, 0:1, :1]
    row = lax.broadcasted_iota(jnp.int32, (sr, 1), 0)
    for b in range(n_req):
        mine = jnp.logical_and(row >= b * SROWS, row < (b + 1) * SROWS)
        s0 = s0_ref[b, 0]
        o = o + _dot(jnp.where(mine, qdec, 0.0).astype(BF16), s0.astype(BF16))
        kb = jnp.transpose(jnp.where(mine, kdec, 0.0)).astype(BF16)
        s_ref[b, 0] = gc * s0 + _dot(kb, vb)
    o_ref[...] = _head_ln_gate(o, gain_ref[...], g_ref[0])


def _ret_sample(proj_s, gain, state0, past_len):
    _, sr, width = proj_s.shape
    nh = width // DV_RET
    nreq = sr // SROWS
    t = jnp.arange(sr) % SROWS
    cos, sin, log_g, _ = _ret_tables(past_len + t, DEC_SEQ, nh)
    tf = t.astype(F32)
    diff = tf[:, None] - tf[None, :]
    same = (jnp.arange(sr)[:, None] // SROWS) == (jnp.arange(sr)[None, :] // SROWS)
    real = (t < DEC_SEQ)
    ok = same & (diff >= 0) & real[:, None] & real[None, :]
    decay = jnp.where(ok[None], jnp.exp(jnp.maximum(diff, 0.0)[None] * log_g[:, None, None]), 0.0)
    ones = jnp.ones((1, 1, DK_RET), F32)
    qd = jnp.exp((tf + 1)[None, :, None] * log_g[:, None, None]) * ones
    kd = jnp.where(real[None, :, None],
                   jnp.exp((DEC_SEQ - 1 - tf)[None, :, None] * log_g[:, None, None]), 0.0) * ones
    gc = jnp.exp(DEC_SEQ * log_g)[:, None, None] * jnp.ones((1, SUBLANES, LANES), F32)
    body = functools.partial(_ret_sample_body, n_req=nreq)
    tab = lambda: pl.BlockSpec((sr, DK_RET), lambda h: (0, 0))
    per_head = lambda r, w: pl.BlockSpec((1, r, w), lambda h: (h, 0, 0))
    st = pl.BlockSpec((nreq, 1, DK_RET, DV_RET), lambda h: (0, h, 0, 0))
    return pl.pallas_call(
        body,
        grid=(nh,),
        in_specs=[
            pl.BlockSpec((1, sr, DK_RET), lambda h: (3, 0, h)),
            pl.BlockSpec((1, sr, DK_RET), lambda h: (3, 0, nh + h)),
            pl.BlockSpec((1, sr, DV_RET), lambda h: (4, 0, h)),
            pl.BlockSpec((1, sr, DV_RET), lambda h: (5, 0, h)),
            tab(), tab(),
            per_head(sr, sr), per_head(sr, DK_RET), per_head(sr, DK_RET), per_head(SUBLANES, LANES),
            pl.BlockSpec((1, DV_RET), lambda h: (0, h)),
            st,
        ],
        out_specs=[pl.BlockSpec((sr, DV_RET), lambda h: (0, h)), st],
        out_shape=[jax.ShapeDtypeStruct((sr, width), F32),
                   jax.ShapeDtypeStruct(state0.shape, F32)],
        compiler_params=_cp(("arbitrary",)),
        name="ret_sample",
    )(proj_s, proj_s, proj_s, proj_s, cos, sin, decay, qd, kd, gc, gain, state0)


def _diff_rope_tables(pos):
    half = ROT_DIM // 2
    inv = np.float32(ROPE_THETA) ** (-np.arange(half, dtype=np.float32) / half)
    ang = pos.astype(F32)[:, None] * jnp.asarray(inv)[None, :]
    cos, sin = jnp.cos(ang), jnp.sin(ang)
    n = pos.shape[0]
    rest = HD_DIFF - ROT_DIM
    c = jnp.concatenate([cos, cos, jnp.ones((n, rest), F32)], axis=1)
    s1 = jnp.concatenate([-sin, jnp.zeros((n, half + rest), F32)], axis=1)
    s2 = jnp.concatenate([jnp.zeros((n, half), F32), sin, jnp.zeros((n, rest), F32)], axis=1)
    two = lambda a: jnp.concatenate([a, a], axis=1)
    return two(c), two(s1), two(s2)


def _seg_mean_mat():
    a = np.arange(2 * HD_DIFF)
    return jnp.asarray((a[:, None] // HD_DIFF == a[None, :] // HD_DIFF).astype(np.float32) / HD_DIFF,
                       dtype=BF16)


def _qk_norm_rope(x, gain, c, s1, s2, seg):
    hi, lo = _split_bf16(x * x)
    ms = _dot(hi, seg) + _dot(lo, seg)
    y = x * lax.rsqrt(ms + EPS) * gain
    half = ROT_DIM // 2
    return y * c + pltpu.roll(y, LANES - half, axis=1) * s1 + pltpu.roll(y, half, axis=1) * s2


def _diff_prep_body(q_ref, k_ref, qg_ref, kg_ref, c_ref, s1_ref, s2_ref, seg_ref, qo_ref, ko_ref,
                    *, n_heads, q_scale):
    c, s1, s2, seg = c_ref[...], s1_ref[...], s2_ref[...], seg_ref[...]
    for h in range(n_heads):
        cols = slice(h * LANES, (h + 1) * LANES)
        q = _qk_norm_rope(q_ref[0, :, cols], qg_ref[...], c, s1, s2, seg)
        qo_ref[:, cols] = (q * q_scale).astype(qo_ref.dtype)
        ko_ref[:, cols] = _qk_norm_rope(k_ref[0, :, cols], kg_ref[...], c, s1, s2, seg)


def _diff_prep(proj, qk_gain, pos, rows_per_pos_table, q_dtype):
    _, rows, width = proj.shape
    nh = width // LANES
    tm = min(256, rows)
    c, s1, s2 = _diff_rope_tables(pos)
    nt = rows_per_pos_table // tm
    qg = jnp.tile(qk_gain[0], 2)[None, :]
    kg = jnp.tile(qk_gain[1], 2)[None, :]
    body = functools.partial(_diff_prep_body, n_heads=nh, q_scale=HD_DIFF ** -0.5 * math.log2(math.e))
    tab = lambda: pl.BlockSpec((tm, LANES), lambda i: (i % nt, 0))
    vec = lambda: pl.BlockSpec((1, LANES), lambda i: (0, 0))
    return pl.pallas_call(
        body,
        grid=(rows // tm,),
        in_specs=[pl.BlockSpec((1, tm, width), lambda i: (0, i, 0)),
                  pl.BlockSpec((1, tm, width), lambda i: (1, i, 0)),
                  vec(), vec(), tab(), tab(), tab(),
                  pl.BlockSpec((LANES, LANES), lambda i: (0, 0))],
        out_specs=[pl.BlockSpec((tm, width), lambda i: (i, 0)),
                   pl.BlockSpec((tm, width), lambda i: (i, 0))],
        out_shape=[jax.ShapeDtypeStruct((rows, width), q_dtype),
                   jax.ShapeDtypeStruct((rows, width), F32)],
        compiler_params=_cp(("arbitrary",)),
        name="diff_prep",
    )(proj, proj, qg, kg, c, s1, s2, _seg_mean_mat())


def _diff_lambda(lam_ref, lam_init):
    lv = lam_ref[...]
    a = jnp.sum(lv[0:1] * lv[1:2], axis=-1, keepdims=True)
    b = jnp.sum(lv[2:3] * lv[3:4], axis=-1, keepdims=True)
    return jnp.exp(a) - jnp.exp(b) + lam_init


def _stack_components(q):
    lane = lax.broadcasted_iota(jnp.int32, q.shape, 1)
    zero = jnp.zeros_like(q)
    return jnp.concatenate([jnp.where(lane < HD_DIFF, q, zero), jnp.where(lane >= HD_DIFF, q, zero)],
                           axis=0)


def _softmax_step(qz, kblk, vblk, m, l, acc, mask):
    s = _dot_nt(qz, kblk)
    if mask is not None:
        s = jnp.where(mask, s, -jnp.inf)
    m_new = jnp.maximum(m, jnp.max(s, axis=-1, keepdims=True))
    alpha = jnp.exp2(m - m_new)
    p = jnp.exp2(s - m_new)
    l = alpha * l + jnp.sum(p, axis=-1, keepdims=True)
    acc = alpha * acc + _dot(p.astype(BF16), vblk)
    return m_new, l, acc


def _diff_finish(l, acc, n, lam, sub_g, out_scale):
    o = acc[:n] / l[:n] - lam * (acc[n:] / l[n:])
    y = o * lax.rsqrt(jnp.mean(o * o, axis=-1, keepdims=True) + EPS) * sub_g
    return y * out_scale


def _diff_prompt_body(q_ref, k_ref, v_ref, lam_ref, subg_ref, o_ref, kb_scr, vt_scr, qt_scr, qz_scr,
                      m_scr, l_scr, acc_scr, *, nq, tile, n_heads, lam_init):
    kb_scr[...] = k_ref[...].astype(BF16)
    head_cols = [slice(h * DV_DIFF, (h + 1) * DV_DIFF) for h in range(n_heads)]
    for h, cols in enumerate(head_cols):
        for j in range(nq):
            blk = slice(j * tile, (j + 1) * tile)
            vt_scr[h, j] = jnp.transpose(v_ref[0, blk, cols]).astype(BF16)
            qt_scr[h, j] = jnp.transpose(q_ref[blk, cols].astype(F32)).astype(BF16)
    lam = _diff_lambda(lam_ref, lam_init)
    key = lax.broadcasted_iota(jnp.int32, (tile, 2 * tile), 0)
    qry = lax.broadcasted_iota(jnp.int32, (tile, 2 * tile), 1) & (tile - 1)
    causal = key <= qry
    first_comp = lax.broadcasted_iota(jnp.int32, (DV_DIFF, tile), 0) < HD_DIFF

    def steps(kb, rk, mask):
        for h, cols in enumerate(head_cols):
            s = _dot(kb_scr[rk, cols], qz_scr[h])
            if mask is not None:
                s = jnp.where(mask, s, -jnp.inf)
            m_old = m_scr[h]
            m_new = jnp.maximum(m_old, jnp.max(s, axis=0, keepdims=True))
            alpha = jnp.exp2(m_old - m_new)
            p = jnp.exp2(s - m_new)
            l_scr[h] = alpha * l_scr[h] + jnp.sum(p, axis=0, keepdims=True)
            acc_scr[h] = alpha * acc_scr[h] + _dot(vt_scr[h, kb], p.astype(BF16))
            m_scr[h] = m_new

    def q_body(qi, c):
        rows = pl.ds(pl.multiple_of(qi * tile, tile), tile)
        for h in range(n_heads):
            qt = qt_scr[h, qi]
            zero = jnp.zeros_like(qt)
            qz_scr[h] = jnp.concatenate([jnp.where(first_comp, qt, zero),
                                         jnp.where(first_comp, zero, qt)], axis=1)
        m_scr[...] = jnp.full(m_scr.shape, -jnp.inf, F32)
        l_scr[...] = jnp.zeros_like(l_scr)
        acc_scr[...] = jnp.zeros_like(acc_scr)

        def kv_body(kb, c2):
            steps(kb, pl.ds(pl.multiple_of(kb * tile, tile), tile), None)
            return c2

        lax.fori_loop(0, qi, kv_body, 0)
        steps(qi, rows, causal)
        for h, cols in enumerate(head_cols):
            ot = acc_scr[h] / l_scr[h]
            d = ot[:, :tile] - lam * ot[:, tile:]
            y = d * lax.rsqrt(jnp.mean(d * d, axis=0, keepdims=True) + EPS) * subg_ref[...]
            o_ref[rows, cols] = jnp.transpose(y * (1.0 - lam_init)).astype(o_ref.dtype)
        return c

    lax.fori_loop(0, nq, q_body, 0)


def _diff_prompt(qn, kn, proj, diff_lambda, sub_g, lam_init):
    tp, width = qn.shape
    nb = tp // SEQ
    nh = width // DV_DIFF
    hpb = min(4, nh)
    tile = min(256, SEQ)
    w = hpb * DV_DIFF
    body = functools.partial(_diff_prompt_body, nq=SEQ // tile, tile=tile, n_heads=hpb,
                             lam_init=lam_init)
    return pl.pallas_call(
        body,
        grid=(nb, nh // hpb),
        in_specs=[pl.BlockSpec((SEQ, w), lambda b, h: (b, h)),
                  pl.BlockSpec((SEQ, w), lambda b, h: (b, h)),
                  pl.BlockSpec((1, SEQ, w), lambda b, h: (2, b, h)),
                  pl.BlockSpec(diff_lambda.shape, lambda b, h: (0, 0)),
                  pl.BlockSpec((DV_DIFF, 1), lambda b, h: (0, 0))],
        out_specs=pl.BlockSpec((SEQ, w), lambda b, h: (b, h)),
        out_shape=jax.ShapeDtypeStruct((tp, width), BF16),
        scratch_shapes=[pltpu.VMEM((SEQ, w), BF16),
                        pltpu.VMEM((hpb, SEQ // tile, DV_DIFF, tile), BF16),
                        pltpu.VMEM((hpb, SEQ // tile, DV_DIFF, tile), BF16),
                        pltpu.VMEM((hpb, DV_DIFF, 2 * tile), BF16),
                        pltpu.VMEM((hpb, 1, 2 * tile), F32), pltpu.VMEM((hpb, 1, 2 * tile), F32),
                        pltpu.VMEM((hpb, DV_DIFF, 2 * tile), F32)],
        compiler_params=_cp(("arbitrary", "arbitrary")),
        name="diff_prompt",
    )(qn, kn, proj, diff_lambda, sub_g[:, None])


def _diff_sample_body(pt_ref, q_ref, kn_ref, vn_ref, bias_ref, *rest, n_heads, n_steps, pages_per_step,
                      lam_init):
    kc_refs = rest[:pages_per_step]
    vc_refs = rest[pages_per_step:2 * pages_per_step]
    lam_ref, subg_ref, o_ref, m_scr, l_scr, acc_scr = rest[2 * pages_per_step:]
    s = pl.program_id(1)
    nq = 2 * DEC_SEQ

    @pl.when(s == 0)
    def _():
        m_scr[...] = jnp.full(m_scr.shape, -jnp.inf, F32)
        l_scr[...] = jnp.zeros_like(l_scr)
        acc_scr[...] = jnp.zeros_like(acc_scr)

    qz = q_ref[0]
    scores = [_dot_nt(qz, kc_ref[0, 0].astype(BF16)) + bias_ref[...] for kc_ref in kc_refs]
    m_old = m_scr[...]
    m_new = m_old
    for sc in scores:
        m_new = jnp.maximum(m_new, jnp.max(sc, axis=-1, keepdims=True))
    alpha = jnp.exp2(m_old - m_new)
    l = alpha * l_scr[...]
    acc = alpha * acc_scr[...]
    for sc, vc_ref in zip(scores, vc_refs):
        p = jnp.exp2(sc - m_new)
        l = l + jnp.sum(p, axis=-1, keepdims=True)
        acc = acc + _dot(p.astype(BF16), vc_ref[0, 0].astype(BF16))
    l_scr[...] = l
    acc_scr[...] = acc
    m_scr[...] = m_new

    @pl.when(s == n_steps - 1)
    def _():
        lam = _diff_lambda(lam_ref, lam_init)
        row = lax.broadcasted_iota(jnp.int32, (nq, ATT_BLK), 0) & (DEC_SEQ - 1)
        col = lax.broadcasted_iota(jnp.int32, (nq, ATT_BLK), 1)
        causal = col <= row
        pad = jnp.zeros((ATT_BLK - SROWS, LANES), F32)
        first_comp = lax.broadcasted_iota(jnp.int32, (nq, DV_DIFF), 0) < DEC_SEQ
        qf = qz.astype(F32)
        for h in range(n_heads):
            cols = slice(h * LANES, (h + 1) * LANES)
            rows = slice(h * nq, (h + 1) * nq)
            kblk = jnp.concatenate([kn_ref[:, cols], pad], axis=0).astype(BF16)
            vblk = jnp.concatenate([vn_ref[0, :, cols], pad], axis=0).astype(BF16)
            _, l, acc = _softmax_step(qf[rows].astype(BF16), kblk, vblk, m_scr[rows, :],
                                      l_scr[rows, :], acc_scr[rows, :], causal)
            o2 = acc / l
            o = o2 - lam * pltpu.roll(o2, DEC_SEQ, axis=0)
            y = o * lax.rsqrt(jnp.mean(o * o, axis=-1, keepdims=True) + EPS) * subg_ref[...]
            o_ref[:, cols] = jnp.where(first_comp, y * (1.0 - lam_init), 0.0)


def _diff_sample(qn_s, kn_s, proj_s, cache_k, cache_v, o, page_table, diff_lambda, sub_g, lam_init):
    sr, width = qn_s.shape
    nh = width // DV_DIFF
    nreq, n_pages = page_table.shape
    assert 2 * DEC_SEQ == SROWS
    pps = 4 if n_pages % 4 == 0 else 1
    n_steps = n_pages // pps
    nq = 2 * DEC_SEQ
    rows = nh * nq
    q4 = jnp.transpose(qn_s.reshape(nreq, SROWS, nh, DV_DIFF)[:, :DEC_SEQ], (0, 2, 1, 3))
    lane = jnp.arange(DV_DIFF)
    qz = jnp.stack([jnp.where(lane < HD_DIFF, q4, 0.0), jnp.where(lane >= HD_DIFF, q4, 0.0)], axis=2)
    qz = qz.reshape(nreq, rows, DV_DIFF).astype(BF16)
    same_head = (np.arange(PAGE_SIZE * nh)[None, :] % nh) == (np.arange(rows)[:, None] // nq)
    bias = jnp.asarray(np.where(same_head, 0.0, -np.inf), F32)
    body = functools.partial(_diff_sample_body, n_heads=nh, n_steps=n_steps, pages_per_step=pps,
                             lam_init=lam_init)
    page = lambda g: pl.BlockSpec((1, 1, PAGE_SIZE * nh, DV_DIFF),
                                  lambda b, s, pt: (o, pt[b, s * pps + g], 0, 0))
    return pl.pallas_call(
        body,
        grid_spec=pltpu.PrefetchScalarGridSpec(
            num_scalar_prefetch=1,
            grid=(nreq, n_steps),
            in_specs=[pl.BlockSpec((1, rows, DV_DIFF), lambda b, s, pt: (b, 0, 0)),
                      pl.BlockSpec((SROWS, width), lambda b, s, pt: (b, 0)),
                      pl.BlockSpec((1, SROWS, width), lambda b, s, pt: (2, b, 0)),
                      pl.BlockSpec(bias.shape, lambda b, s, pt: (0, 0))]
                     + [page(g) for g in range(pps)] + [page(g) for g in range(pps)]
                     + [pl.BlockSpec(diff_lambda.shape, lambda b, s, pt: (0, 0)),
                        pl.BlockSpec((1, DV_DIFF), lambda b, s, pt: (0, 0))],
            out_specs=pl.BlockSpec((SROWS, width), lambda b, s, pt: (b, 0)),
            scratch_shapes=[pltpu.VMEM((rows, 1), F32), pltpu.VMEM((rows, 1), F32),
                            pltpu.VMEM((rows, DV_DIFF), F32)],
        ),
        out_shape=jax.ShapeDtypeStruct((sr, width), F32),
        compiler_params=_cp(("arbitrary", "arbitrary")),
        name="diff_sample",
    )(page_table, qz, kn_s, proj_s, bias, *([cache_k] * pps), *([cache_v] * pps),
      diff_lambda, sub_g[None, :])


def _route(h, whi_ref, wlo_ref, b_ref, ltri_ref, cnt_scr, row_real):
    n = h.shape[0]
    hi, lo = _split_bf16(h)
    logits = _dot(hi, whi_ref[...]) + _dot(lo, whi_ref[...]) + _dot(hi, wlo_ref[...]) + b_ref[...]
    lane = lax.broadcasted_iota(jnp.int32, logits.shape, 1).astype(F32)
    neg = -jnp.inf
    first = lambda hit: jnp.min(jnp.where(hit, lane, float(LANES)), axis=-1, keepdims=True)
    gl = jnp.where(lane < N_GROUPS, logits, neg)
    gmax = jnp.max(gl, axis=-1, keepdims=True)
    g_top = 1.0 / jnp.sum(jnp.exp(gl - gmax), axis=-1, keepdims=True)
    g_idx = first(gl == gmax)
    lo_lane = N_GROUPS + EXP_PER_GROUP * g_idx
    el = jnp.where(jnp.logical_and(lane >= lo_lane, lane < lo_lane + EXP_PER_GROUP), logits, neg)
    emax = jnp.max(el, axis=-1, keepdims=True)
    esum = jnp.sum(jnp.exp(el - emax), axis=-1, keepdims=True)
    l1 = first(el == emax)
    el2 = jnp.where(lane == l1, neg, el)
    e2max = jnp.max(el2, axis=-1, keepdims=True)
    l2 = first(el2 == e2max)
    p1 = 1.0 / esum
    p2 = jnp.exp(e2max - emax) / esum
    w1 = p1 / (p1 + p2) * g_top
    w2 = p2 / (p1 + p2) * g_top
    e1 = l1 - N_GROUPS
    e2 = l2 - N_GROUPS
    hit1 = lane == e1
    hit2 = lane == e2
    both = jnp.logical_or(hit1, hit2)
    if row_real is not None:
        both = jnp.logical_and(both, row_real)
    both = jnp.where(both, 1.0, 0.0)
    before = _dot(ltri_ref[:n, :n], both.astype(BF16)) + cnt_scr[...]
    rank1 = jnp.sum(jnp.where(hit1, before, 0.0), axis=-1, keepdims=True)
    rank2 = jnp.sum(jnp.where(hit2, before, 0.0), axis=-1, keepdims=True)
    cnt_scr[...] += jnp.sum(both, axis=0, keepdims=True)
    rec = jnp.where(lane == 0, e1, jnp.where(lane == 1, e2, jnp.where(lane == 2, w1, jnp.where(
        lane == 3, w2, jnp.where(lane == 4, rank1, jnp.where(lane == 5, rank2, 0.0))))))
    return hi, rec


def _router_body(xp_ref, xs_ref, g_ref, sh_ref, sc_ref, shs_ref, scs_ref, whi_ref, wlo_ref, b_ref,
                 ltri_ref, h_ref, rec_ref, cnt_ref, cnt_scr, *, n_p, tiles_per_batch):
    i = pl.program_id(0)
    tm = xp_ref.shape[0]
    sr = xs_ref.shape[0]

    @pl.when(i == 0)
    def _():
        cnt_scr[...] = jnp.zeros_like(cnt_scr)

    @pl.when(i < n_p)
    def _():
        b = i // tiles_per_batch
        sh = sh_ref[0, 0, pl.ds(b, 1), :]
        sc = sc_ref[0, 0, pl.ds(b, 1), :]
        chunk = min(256, tm)

        def body(r, c):
            sl = pl.ds(pl.multiple_of(r * chunk, chunk), chunk)
            hi, rec = _route(_modulate(xp_ref[sl, :], g_ref[...], sh, sc), whi_ref, wlo_ref, b_ref,
                             ltri_ref, cnt_scr, None)
            h_ref[sl, :] = hi
            rec_ref[sl, :] = rec
            return c

        lax.fori_loop(0, tm // chunk, body, 0)

    @pl.when(i == n_p)
    def _():
        hi, rec = _route(_mod_sample_tile(xs_ref, g_ref[...], shs_ref, scs_ref), whi_ref, wlo_ref, b_ref,
                         ltri_ref, cnt_scr, _sample_row_valid(sr))
        h_ref[:sr, :] = hi
        rec_ref[:sr, :] = rec
        h_ref[sr:, :] = jnp.zeros((tm - sr, h_ref.shape[1]), h_ref.dtype)
        rec_ref[sr:, :] = jnp.zeros((tm - sr, LANES), F32)

    @pl.when(i > n_p)
    def _():
        h_ref[...] = jnp.zeros_like(h_ref)
        rec_ref[...] = jnp.zeros_like(rec_ref)

    cnt_ref[...] = jnp.broadcast_to(cnt_scr[...], cnt_ref.shape)


def _router(xp, xs, g, mods, li, mod_s, w_rg, b_rg, w_re, b_re, min_rows):
    tp, d = xp.shape
    sr = xs.shape[0]
    tm = _prompt_tile()
    n_p = tp // tm
    last = n_p - 1
    wr = jnp.concatenate([w_rg, jnp.moveaxis(w_re, 0, 1).reshape(d, N_EXPERTS),
                          jnp.zeros((d, LANES - N_GROUPS - N_EXPERTS), F32)], axis=1)
    whi = wr.astype(BF16)
    wlo = (wr - whi.astype(F32)).astype(BF16)
    bias = jnp.concatenate([b_rg, b_re.reshape(-1), jnp.zeros((LANES - N_GROUPS - N_EXPERTS,), F32)])[None]
    body = functools.partial(_router_body, n_p=n_p, tiles_per_batch=SEQ // tm)
    full = lambda r, c: pl.BlockSpec((r, c), lambda i: (0, 0))
    rows = max(tp + tm, -(-min_rows // tm) * tm)
    chunk = min(256, tm)
    ltri = jnp.asarray(np.tril(np.ones((chunk, chunk), np.float32), -1), dtype=BF16)
    return pl.pallas_call(
        body,
        grid=(rows // tm,),
        in_specs=[pl.BlockSpec((tm, d), lambda i: (jnp.minimum(i, last), 0)),
                  full(sr, d),
                  pl.BlockSpec((1, d), lambda i: (0, 0)),
                  pl.BlockSpec((1, 1, MOD_ROWS, d), lambda i: (li, 3, 0, 0)),
                  pl.BlockSpec((1, 1, MOD_ROWS, d), lambda i: (li, 4, 0, 0)),
                  full(sr, d), full(sr, d), full(d, LANES), full(d, LANES), full(1, LANES),
                  full(chunk, chunk)],
        out_specs=[pl.BlockSpec((tm, d), lambda i: (i, 0)),
                   pl.BlockSpec((tm, LANES), lambda i: (i, 0)),
                   full(SUBLANES, LANES)],
        out_shape=[jax.ShapeDtypeStruct((rows, d), BF16),
                   jax.ShapeDtypeStruct((rows, LANES), F32),
                   jax.ShapeDtypeStruct((SUBLANES, LANES), F32)],
        scratch_shapes=[pltpu.VMEM((1, LANES), F32)],
        compiler_params=_cp(("arbitrary",)),
        name="router",
    )(xp, xs, g, mods, mods, mod_s[3], mod_s[4], whi, wlo, bias, ltri)


def _experts_body(te_ref, nv_ref, *refs, starts):
    x_refs = refs[:len(starts)]
    wi_ref, wo_ref, y_ref, wi_scr, wo_scr = refs[len(starts):]
    t = pl.program_id(0)
    f = wo_ref.shape[2]
    fresh = jnp.logical_or(t == 0, te_ref[t] != te_ref[jnp.maximum(t - 1, 0)])

    @pl.when(jnp.logical_and(t < nv_ref[0], fresh))
    def _():
        wi_scr[...] = wi_ref[0, 0].astype(BF16)
        wo_scr[...] = wo_ref[0, 0].astype(BF16)

    @pl.when(t < nv_ref[0])
    def _():
        x = x_refs[0][...]
        for start, x_ref in zip(starts[1:], x_refs[1:]):
            x = jnp.where(t >= start, x_ref[...], x)
        hid = _dot(x, wi_scr[...])
        gate = hid[:, f:]
        act = (gate * jax.nn.sigmoid(gate)) * hid[:, :f]
        y_ref[...] = _dot(act.astype(BF16), wo_scr[...]).astype(y_ref.dtype)

    @pl.when(t >= nv_ref[0])
    def _():
        y_ref[...] = jnp.zeros_like(y_ref)


def _experts(x_parts, tile_expert, n_valid, w_in, w_out, li):
    d = x_parts[0].shape[1]
    sizes = [x.shape[0] // EXP_TILE for x in x_parts]
    starts = tuple(int(s) for s in np.cumsum([0] + sizes[:-1]))
    p = sum(sizes) * EXP_TILE
    f2 = w_in.shape[3]

    def part(start, size):
        return pl.BlockSpec((EXP_TILE, d), lambda t, te, nv: (jnp.clip(t - start, 0, size - 1), 0))

    return pl.pallas_call(
        functools.partial(_experts_body, starts=starts),
        grid_spec=pltpu.PrefetchScalarGridSpec(
            num_scalar_prefetch=2,
            grid=(p // EXP_TILE,),
            in_specs=[part(s, n) for s, n in zip(starts, sizes)]
                     + [pl.BlockSpec((1, 1, d, f2), lambda t, te, nv: (li, te[t], 0, 0)),
                      pl.BlockSpec((1, 1, f2 // 2, d), lambda t, te, nv: (li, te[t], 0, 0))],
            out_specs=pl.BlockSpec((EXP_TILE, d), lambda t, te, nv: (t, 0)),
            scratch_shapes=[pltpu.VMEM((d, f2), BF16), pltpu.VMEM((f2 // 2, d), BF16)],
        ),
        out_shape=jax.ShapeDtypeStruct((p, d), BF16),
        compiler_params=_cp(("arbitrary",)),
        name="experts",
    )(tile_expert, n_valid, *x_parts, w_in, w_out)


def _combine_body(y1p_ref, y2p_ref, y1s_ref, y2s_ref, recp_ref, recs_ref, xp_ref, xs_ref,
                  gate_ref, gates_ref, op_ref, os_ref, *, n_p, tiles_per_batch):
    i = pl.program_id(0)

    def mix(y1, y2, rec):
        return rec[:, 2:3] * y1.astype(F32) + rec[:, 3:4] * y2.astype(F32)

    @pl.when(i < n_p)
    def _():
        b = i // tiles_per_batch
        gate = gate_ref[0, 0, pl.ds(b, 1), :]
        op_ref[...] = xp_ref[...] + gate * mix(y1p_ref[0], y2p_ref[0], recp_ref[...])

    @pl.when(i == n_p)
    def _():
        os_ref[...] = xs_ref[...] + gates_ref[...] * mix(y1s_ref[0], y2s_ref[0], recs_ref[...])


def _combine(yg, rec, xp, xs, mods, li, mod_s):
    tp, d = xp.shape
    sr = xs.shape[0]
    tm = min(512, SEQ)
    n_p = tp // tm
    last = n_p - 1
    stile = tp // sr
    body = functools.partial(_combine_body, n_p=n_p, tiles_per_batch=SEQ // tm)
    pmap = lambda i: (jnp.minimum(i, last), 0)
    return pl.pallas_call(
        body,
        grid=(n_p + 1,),
        in_specs=[pl.BlockSpec((1, tm, d), lambda i: (0, jnp.minimum(i, last), 0)),
                  pl.BlockSpec((1, tm, d), lambda i: (1, jnp.minimum(i, last), 0)),
                  pl.BlockSpec((1, sr, d), lambda i: (0, stile, 0)),
                  pl.BlockSpec((1, sr, d), lambda i: (1, stile, 0)),
                  pl.BlockSpec((tm, LANES), pmap),
                  pl.BlockSpec((sr, LANES), lambda i: (stile, 0)),
                  pl.BlockSpec((tm, d), pmap),
                  pl.BlockSpec((sr, d), lambda i: (0, 0)),
                  pl.BlockSpec((1, 1, MOD_ROWS, d), lambda i: (li, 5, 0, 0)),
                  pl.BlockSpec((sr, d), lambda i: (0, 0))],
        out_specs=[pl.BlockSpec((tm, d), pmap), pl.BlockSpec((sr, d), lambda i: (0, 0))],
        out_shape=[jax.ShapeDtypeStruct((tp, d), F32), jax.ShapeDtypeStruct((sr, d), F32)],
        compiler_params=_cp(("arbitrary",)),
        name="combine",
    )(yg, yg, yg, yg, rec, rec, xp, xs, mods, mod_s[5])


def _moe(xp, xs, g, mods, li, mod_s, w_rg, b_rg, w_re, b_re, w_in, w_out):
    tp, d = xp.shape
    sr = xs.shape[0]
    n_real = 2 * (tp + (sr // SROWS) * DEC_SEQ)
    p = (n_real + N_EXPERTS * (EXP_TILE - 1) + EXP_TILE - 1) // EXP_TILE * EXP_TILE
    h_all, rec, counts = _router(xp, xs, g, mods, li, mod_s, w_rg, b_rg, w_re, b_re, p + 1024)
    nt = tp + sr
    zero_row = nt
    row = jnp.arange(nt)
    real = jnp.logical_or(row < tp, (row - tp) % SROWS < DEC_SEQ)
    ids = rec[:nt, :6].astype(jnp.int32)
    eid, rank = ids[:, 0:2], ids[:, 4:6]
    count = counts[0, :N_EXPERTS].astype(jnp.int32)
    padded = (count + EXP_TILE - 1) // EXP_TILE * EXP_TILE
    ends = jnp.cumsum(padded)
    start = ends - padded
    pos =jnp.where(real[:, None], start[eid] + rank, p).T.reshape(-1)
    src = jnp.full((p,), zero_row, jnp.int32).at[pos].set(jnp.tile(row, 2), mode="drop")
    tile_start = jnp.arange(p // EXP_TILE) * EXP_TILE
    tile_expert = jnp.minimum(jnp.sum((ends[None, :] <= tile_start[:, None]).astype(jnp.int32), axis=1),
                              N_EXPERTS - 1)
    n_valid = (ends[-1] // EXP_TILE).astype(jnp.int32)[None]
    x_parts = [jnp.take(h_all, src, axis=0, mode="clip")]
    ys = _experts(x_parts, tile_expert, n_valid, w_in, w_out, li)
    yg = jnp.take(ys, jnp.minimum(pos, p - 1), axis=0, mode="clip").reshape(2, nt, d)
    return _combine(yg, rec, xp, xs, mods, li, mod_s)


def kernel(x_prompt, x_sample, cache_sb_k, cache_sb_v, state_ret, cache_diff_k, cache_diff_v, page_table, c_prompt, c_sample, w_ada, b_ada, norm_mix, norm_ffn, w_in_even, w_out_even, ret_gn_gain, w_in_odd, w_out_odd, qk_gain, diff_lambda, diff_subln, w_router_group, b_router_group, w_router_expert, b_router_expert, w_expert_in, w_expert_out):
    nb, seq, d = x_prompt.shape
    nreq, dec_seq, _ = x_sample.shape
    n_pages = page_table.shape[1]
    past_len = n_pages * PAGE_SIZE
    tp = nb * seq
    sr = nreq * SROWS
    h_sb = cache_sb_k.shape[3]
    h_diff = cache_diff_k.shape[3]

    xp = x_prompt.reshape(tp, d)
    xs = jnp.pad(x_sample, ((0, 0), (0, SROWS - dec_seq), (0, 0))).reshape(sr, d)
    c_all = jnp.concatenate([c_prompt, c_sample, jnp.zeros((MOD_ROWS - nb - nreq, d), F32)], axis=0)
    mods = _ada(c_all, w_ada, b_ada)
    uu = _cumsum_mat(ATT_BLK)
    csk = cache_sb_k.reshape(cache_sb_k.shape[0], cache_sb_k.shape[1], PAGE_SIZE * h_sb, HD_SB)
    csv = cache_sb_v.reshape(csk.shape)
    cdk = cache_diff_k.reshape(cache_diff_k.shape[0], cache_diff_k.shape[1], PAGE_SIZE * h_diff, DV_DIFF)
    cdv = cache_diff_v.reshape(cdk.shape)
    pos_s = past_len + (jnp.arange(sr) % SROWS)

    def sample_rows(a, tail):
        return a.reshape(nreq, SROWS, *tail)[:, :dec_seq]

    sbk_p, sbv_p, sbk_s, sbv_s, ret_p, ret_s = [], [], [], [], [], []
    dk_p, dv_p, dk_s, dv_s = [], [], [], []
    for li in range(DEPTH):
        mod_s = [jnp.repeat(mods[li, k, nb:nb + nreq], SROWS, axis=0) for k in range(ADA_MOD)]
        if li % 2 == 0:
            e = li // 2
            w_in = w_in_even[e].astype(BF16)
            proj_p, proj_s = _inproj(xp, xs, norm_mix[li][None], mods, li, 0, 1, mod_s, w_in, h_sb * HD_SB)
            gain = ret_gn_gain[e][None, :]
            o_sb_p = _sb_prompt(proj_p, uu)
            o_r_p, s_p = _ret_prompt(proj_p, gain)
            o_sb_s = _sb_sample(proj_s, csk, csv, e, page_table, uu)
            o_r_s, s_s = _ret_sample(proj_s, gain, state_ret[e], past_len)
            xp, xs = _outproj((o_sb_p, o_r_p), (o_sb_s, o_r_s), w_out_even[e].astype(BF16),
                              xp, xs, mods, li, 2, mod_s)
            sbk_p.append(proj_p[1])
            sbv_p.append(proj_p[2])
            sbk_s.append(sample_rows(proj_s[1], (h_sb, HD_SB)))
            sbv_s.append(sample_rows(proj_s[2], (h_sb, HD_SB)))
            ret_p.append(s_p)
            ret_s.append(s_s)
        else:
            o = li // 2
            lam_init = 0.8 - 0.6 * math.exp(-0.3 * li)
            w_in = w_in_odd[o].astype(BF16)
            proj_p, proj_s = _inproj(xp, xs, norm_mix[li][None], mods, li, 0, 1, mod_s, w_in, h_diff * DV_DIFF)
            qn_p, kn_p = _diff_prep(proj_p, qk_gain[o], jnp.arange(seq), seq, BF16)
            qn_s, kn_s = _diff_prep(proj_s, qk_gain[o], pos_s, sr, F32)
            o_p = _diff_prompt(qn_p, kn_p, proj_p, diff_lambda[o], diff_subln[o], lam_init)
            o_s = _diff_sample(qn_s, kn_s, proj_s, cdk, cdv, o, page_table, diff_lambda[o],
                               diff_subln[o], lam_init)
            xp, xs = _outproj(o_p, o_s, w_out_odd[o].astype(BF16), xp, xs, mods, li, 2, mod_s)
            dk_p.append(kn_p)
            dv_p.append(proj_p[2])
            dk_s.append(sample_rows(kn_s, (h_diff, DV_DIFF)))
            dv_s.append(sample_rows(proj_s[2], (h_diff, DV_DIFF)))
        xp, xs = _moe(xp, xs, norm_ffn[li][None], mods, li, mod_s, w_router_group[li], b_router_group[li],
                      w_router_expert[li], b_router_expert[li], w_expert_in, w_expert_out)
    def heads(rows_list, n_heads, width):
        return jnp.stack(rows_list).reshape(len(rows_list), nb, seq, n_heads, width)

    return (xp.reshape(nb, seq, d), sample_rows(xs, (d,)),
            heads(sbk_p, h_sb, HD_SB), heads(sbv_p, h_sb, HD_SB), jnp.stack(sbk_s), jnp.stack(sbv_s),
            jnp.stack(ret_p), jnp.stack(ret_s), heads(dk_p, h_diff, DV_DIFF), heads(dv_p, h_diff, DV_DIFF),
            jnp.stack(dk_s), jnp.stack(dv_s))
```

```python
import functools
import math

import jax
import jax.numpy as jnp
import numpy as np
from jax import lax
from jax.experimental import pallas as pl
from jax.experimental.pallas import tpu as pltpu

D_MODEL = 2048
BATCH = 4
SEQ = 2048
DEPTH = 4
DEC_BATCH = 8
DEC_SEQ = 4
PAGE_SIZE = 128
EPS = 1e-6
HD_SB = 128
DK_RET = 128
DV_RET = 256
RET_THETA = 10000.0
HD_DIFF = 64
DV_DIFF = 2 * HD_DIFF
ROPE_THETA = 500000.0
ROT_DIM = HD_DIFF // 4
N_GROUPS = 4
EXP_PER_GROUP = 4
N_EXPERTS = N_GROUPS * EXP_PER_GROUP
D_EXPERT = 512
ADA_MOD = 6

F32 = jnp.float32
BF16 = jnp.bfloat16

LANES = 128
SUBLANES = 8
MOD_ROWS = 16
SROWS = SUBLANES
ATT_BLK = 128
EXP_TILE = 256
SB_EXIT = -104.0
VMEM_LIMIT = 56 * 2**20


def _cp(sem, vmem=VMEM_LIMIT):
    return pltpu.CompilerParams(dimension_semantics=sem, vmem_limit_bytes=vmem)


def _dot(a, b):
    return jnp.dot(a, b, preferred_element_type=F32)


def _dot_nt(a, b):
    return lax.dot_general(a, b, (((1,), (1,)), ((), ())), preferred_element_type=F32)


def _split_bf16(x):
    hi = x.astype(BF16)
    lo = (x - hi.astype(F32)).astype(BF16)
    return hi, lo


def _modulate(x, g, shift, scale):
    y = x * lax.rsqrt(jnp.mean(x * x, axis=-1, keepdims=True) + EPS) * g
    return y * (1.0 + scale) + shift


def _sample_row_valid(rows):
    r = lax.broadcasted_iota(jnp.int32, (rows, 1), 0)
    return (r & (SROWS - 1)) < DEC_SEQ


def _ada_body(c_ref, w_ref, b_ref, o_ref):
    c = c_ref[...]
    a = (c * jax.nn.sigmoid(c)).astype(BF16)
    o_ref[0, 0] = _dot(a, w_ref[0].astype(BF16)) + b_ref[0]


def _ada(c_all, w_ada, b_ada):
    depth, d, _ = w_ada.shape
    tn = min(1024, d)
    nj = d // tn
    return pl.pallas_call(
        _ada_body,
        grid=(depth, ADA_MOD, nj),
        in_specs=[
            pl.BlockSpec((MOD_ROWS, d), lambda l, k, j: (0, 0)),
            pl.BlockSpec((1, d, tn), lambda l, k, j: (l, 0, k * nj + j)),
            pl.BlockSpec((1, 1, tn), lambda l, k, j: (l, 0, k * nj + j)),
        ],
        out_specs=pl.BlockSpec((1, 1, MOD_ROWS, tn), lambda l, k, j: (l, k, 0, j)),
        out_shape=jax.ShapeDtypeStruct((depth, ADA_MOD, MOD_ROWS, d), F32),
        compiler_params=_cp(("arbitrary",) * 3),
        name="ada",
    )(c_all, w_ada, b_ada.reshape(depth, 1, -1))


def _prompt_tile():
    return min(1024, SEQ)


def _mod_prompt_tile(x_ref, h_ref, g, sh, sc, rows):
    chunk = min(256, rows)

    def body(r, c):
        sl = pl.ds(pl.multiple_of(r * chunk, chunk), chunk)
        h_ref[sl, :] = _modulate(x_ref[sl, :], g, sh, sc).astype(h_ref.dtype)
        return c

    lax.fori_loop(0, rows // chunk, body, 0)


def _mod_sample_tile(xs_ref, g, shs_ref, scs_ref):
    h = _modulate(xs_ref[...], g, shs_ref[...], scs_ref[...])
    return jnp.where(_sample_row_valid(xs_ref.shape[0]), h, 0.0)


def _inproj_body(xp_ref, xs_ref, g_ref, sh_ref, sc_ref, shs_ref, scs_ref, w_ref,
                 op_ref, os_ref, hp_scr, hs_scr, *, n_p, tiles_per_batch):
    i = pl.program_id(0)
    j = pl.program_id(1)
    tm = xp_ref.shape[0]

    @pl.when(jnp.logical_and(i < n_p, j == 0))
    def _():
        b = i // tiles_per_batch
        _mod_prompt_tile(xp_ref, hp_scr, g_ref[...], sh_ref[0, 0, pl.ds(b, 1), :],
                         sc_ref[0, 0, pl.ds(b, 1), :], tm)

    @pl.when(jnp.logical_and(i == n_p, j == 0))
    def _():
        hs_scr[...] = _mod_sample_tile(xs_ref, g_ref[...], shs_ref, scs_ref).astype(BF16)

    @pl.when(i < n_p)
    def _():
        chunk = min(512, tm)

        def body(r, c):
            sl = pl.ds(pl.multiple_of(r * chunk, chunk), chunk)
            op_ref[0, sl, :] = _dot(hp_scr[sl, :], w_ref[...])
            return c

        lax.fori_loop(0, tm // chunk, body, 0)

    @pl.when(i == n_p)
    def _():
        os_ref[0] = _dot(hs_scr[...], w_ref[...])


def _inproj(xp, xs, g, mods, li, k_shift, k_scale, mod_s, w, group_w):
    tp, d = xp.shape
    sr = xs.shape[0]
    n = w.shape[1]
    tm = _prompt_tile()
    n_p = tp // tm
    tn = min(1024, group_w)
    nj = n // tn
    tpg = group_w // tn
    last = n_p - 1

    def p_out(i, j):
        jj = jnp.where(i == n_p, nj - 1, j)
        return (jj // tpg, jnp.minimum(i, last), jj % tpg)

    def s_out(i, j):
        jj = jnp.where(i == n_p, j, 0)
        return (jj // tpg, 0, jj % tpg)

    body = functools.partial(_inproj_body, n_p=n_p, tiles_per_batch=SEQ // tm)
    return pl.pallas_call(
        body,
        grid=(n_p + 1, nj),
        in_specs=[
            pl.BlockSpec((tm, d), lambda i, j: (jnp.minimum(i, last), 0)),
            pl.BlockSpec((sr, d), lambda i, j: (0, 0)),
            pl.BlockSpec((1, d), lambda i, j: (0, 0)),
            pl.BlockSpec((1, 1, MOD_ROWS, d), lambda i, j: (li, k_shift, 0, 0)),
            pl.BlockSpec((1, 1, MOD_ROWS, d), lambda i, j: (li, k_scale, 0, 0)),
            pl.BlockSpec((sr, d), lambda i, j: (0, 0)),
            pl.BlockSpec((sr, d), lambda i, j: (0, 0)),
            pl.BlockSpec((d, tn), lambda i, j: (0, j)),
        ],
        out_specs=[
            pl.BlockSpec((1, tm, tn), p_out),
            pl.BlockSpec((1, sr, tn), s_out),
        ],
        out_shape=[
            jax.ShapeDtypeStruct((n // group_w, tp, group_w), F32),
            jax.ShapeDtypeStruct((n // group_w, sr, group_w), F32),
        ],
        scratch_shapes=[pltpu.VMEM((tm, d), BF16), pltpu.VMEM((sr, d), BF16)],
        compiler_params=_cp(("arbitrary", "arbitrary")),
        name="inproj",
    )(xp, xs, g, mods, mods, mod_s[k_shift], mod_s[k_scale], w)


def _outproj_body(a1p_ref, a2p_ref, a1s_ref, a2s_ref, w1_ref, w2_ref, xp_ref, xs_ref,
                  gate_ref, gates_ref, op_ref, os_ref, *, n_p, tiles_per_batch):
    i = pl.program_id(0)
    tm = xp_ref.shape[0]

    @pl.when(i < n_p)
    def _():
        b = i // tiles_per_batch
        gate = gate_ref[0, 0, pl.ds(b, 1), :]
        chunk = min(512, tm)

        def body(r, c):
            sl = pl.ds(pl.multiple_of(r * chunk, chunk), chunk)
            y = _dot(a1p_ref[sl, :], w1_ref[...]) + _dot(a2p_ref[sl, :], w2_ref[...])
            op_ref[sl, :] = xp_ref[sl, :] + gate * y
            return c

        lax.fori_loop(0, tm // chunk, body, 0)

    @pl.when(i == n_p)
    def _():
        y = (_dot(a1s_ref[...].astype(BF16), w1_ref[...])
             + _dot(a2s_ref[...].astype(BF16), w2_ref[...]))
        os_ref[...] = xs_ref[...] + gates_ref[...] * y


def _outproj(ap, as_, w, xp, xs, mods, li, k_gate, mod_s):
    tp, d = xp.shape
    sr = xs.shape[0]
    tm = _prompt_tile()
    n_p = tp // tm
    tn = min(1024, d)
    nj = d // tn
    last = n_p - 1
    half = w.shape[0] // 2
    if isinstance(ap, tuple):
        a1p, a2p, a1s, a2s = ap[0], ap[1], as_[0], as_[1]
        c2 = 0
    else:
        a1p = a2p = ap
        a1s = a2s = as_
        c2 = 1

    def pin(c):
        return lambda i, j: (jnp.minimum(i, last), c)

    def p_out(i, j):
        return (jnp.minimum(i, last), jnp.where(i == n_p, nj - 1, j))

    def s_out(i, j):
        return (0, jnp.where(i == n_p, j, 0))

    body = functools.partial(_outproj_body, n_p=n_p, tiles_per_batch=SEQ // tm)
    return pl.pallas_call(
        body,
        grid=(n_p + 1, nj),
        in_specs=[
            pl.BlockSpec((tm, half), pin(0)),
            pl.BlockSpec((tm, half), pin(c2)),
            pl.BlockSpec((sr, half), lambda i, j: (0, 0)),
            pl.BlockSpec((sr, half), lambda i, j: (0, c2)),
            pl.BlockSpec((half, tn), lambda i, j: (0, j)),
            pl.BlockSpec((half, tn), lambda i, j: (1, j)),
            pl.BlockSpec((tm, tn), lambda i, j: (jnp.minimum(i, last), j)),
            pl.BlockSpec((sr, tn), lambda i, j: (0, j)),
            pl.BlockSpec((1, 1, MOD_ROWS, tn), lambda i, j: (li, k_gate, 0, j)),
            pl.BlockSpec((sr, tn), lambda i, j: (0, j)),
        ],
        out_specs=[pl.BlockSpec((tm, tn), p_out), pl.BlockSpec((sr, tn), s_out)],
        out_shape=[jax.ShapeDtypeStruct((tp, d), F32), jax.ShapeDtypeStruct((sr, d), F32)],
        compiler_params=_cp(("arbitrary", "arbitrary")),
        name="outproj",
    )(a1p, a2p, a1s, a2s, w, w, xp, xs, mods, mod_s[k_gate])


def _cumsum_mat(n):
    j = np.arange(n)[:, None]
    s = np.arange(n)[None, :]
    blk = np.concatenate([(j >= s).astype(np.float32), np.ones((n, n), np.float32)], axis=1)
    return jnp.asarray(np.concatenate([blk, blk], axis=0), dtype=BF16)


def _sb_block(qb, kblk, vblk, uu, carry, mask, scale):
    n = kblk.shape[0]
    z = _dot_nt(qb, kblk) * scale
    lk = -(jnp.maximum(z, 0.0) + jnp.log1p(jnp.exp(-jnp.abs(z))))
    if mask is not None:
        lk = jnp.where(mask, lk, 0.0)
    hi, lo = _split_bf16(lk)
    incl = [None] * (n // ATT_BLK)
    for g in reversed(range(n // ATT_BLK)):
        cols = slice(g * ATT_BLK, (g + 1) * ATT_BLK)
        r = _dot(jnp.concatenate([hi[:, cols], lo[:, cols]], axis=1), uu)
        incl[g] = r[:, :ATT_BLK] + carry
        carry = carry + r[:, ATT_BLK:]
    a = jnp.exp(z + jnp.concatenate(incl, axis=1))
    if mask is not None:
        a = jnp.where(mask, a, 0.0)
    return carry, _dot(a.astype(BF16), vblk)


def _sb_prompt_body(q_ref, k_ref, v_ref, uu_ref, o_ref, kb_scr, vb_scr, carry_scr, acc_scr,
                    *, nq, tile, n_heads, scale):
    kb_scr[...] = k_ref[0].astype(BF16)
    vb_scr[...] = v_ref[0].astype(BF16)
    row = lax.broadcasted_iota(jnp.int32, (tile, tile), 0)
    col = lax.broadcasted_iota(jnp.int32, (tile, tile), 1)
    strict = col < row
    uu = uu_ref[...]
    head_cols = [slice(h * HD_SB, (h + 1) * HD_SB) for h in range(n_heads)]

    def q_body(qi, c):
        rows = pl.ds(pl.multiple_of(qi * tile, tile), tile)

        def blocks(rk, first):
            mx = None
            for h, cols in enumerate(head_cols):
                qb = q_ref[0, rows, cols].astype(BF16)
                carry0 = jnp.zeros((tile, ATT_BLK), F32) if first else carry_scr[h]
                carry, o = _sb_block(qb, kb_scr[rk, cols], vb_scr[rk, cols], uu, carry0,
                                     strict if first else None, scale)
                carry_scr[h] = carry
                acc_scr[h] = o if first else acc_scr[h] + o
                top = jnp.max(carry)
                mx = top if mx is None else jnp.maximum(mx, top)
            return mx

        def cond(st):
            kb, mx = st
            return jnp.logical_and(kb >= 0, mx > SB_EXIT)

        def body(st):
            kb, _ = st
            return kb - 1, blocks(pl.ds(pl.multiple_of(kb * tile, tile), tile), False)

        lax.while_loop(cond, body, (qi - 1, blocks(rows, True)))
        for h, cols in enumerate(head_cols):
            o_ref[rows, cols] = acc_scr[h].astype(o_ref.dtype)
        return c

    lax.fori_loop(0, nq, q_body, 0)


def _sb_prompt(proj, uu):
    _, tp, width = proj.shape
    nb = tp // SEQ
    nh = width // HD_SB
    hpb = min(2, nh)
    tile = min(256, SEQ)
    w = hpb * HD_SB
    body = functools.partial(_sb_prompt_body, nq=SEQ // tile, tile=tile, n_heads=hpb,
                             scale=HD_SB ** -0.5)
    spec = lambda g: pl.BlockSpec((1, SEQ, w), lambda b, h: (g, b, h))
    return pl.pallas_call(
        body,
        grid=(nb, nh // hpb),
        in_specs=[spec(0), spec(1), spec(2),
                  pl.BlockSpec((2 * ATT_BLK, 2 * ATT_BLK), lambda b, h: (0, 0))],
        out_specs=pl.BlockSpec((SEQ, w), lambda b, h: (b, h)),
        out_shape=jax.ShapeDtypeStruct((tp, width), BF16),
        scratch_shapes=[pltpu.VMEM((SEQ, w), BF16), pltpu.VMEM((SEQ, w), BF16),
                        pltpu.VMEM((hpb, tile, ATT_BLK), F32), pltpu.VMEM((hpb, tile, HD_SB), F32)],
        compiler_params=_cp(("arbitrary", "arbitrary")),
        name="sb_prompt",
    )(proj, proj, proj, uu)


def _sb_sample_body(pt_ref, q_ref, kn_ref, vn_ref, kc_hbm, vc_hbm, uu_ref, o_ref,
                    kbuf, vbuf, sem, carry_scr, acc_scr, *, layer, n_heads, n_pages, scale):
    b = pl.program_id(0)
    blk = ATT_BLK
    uu = uu_ref[...]
    row = lax.broadcasted_iota(jnp.int32, (SROWS, blk), 0)
    col = lax.broadcasted_iota(jnp.int32, (SROWS, blk), 1)
    real_row = row < DEC_SEQ
    strict = col < row
    pad = jnp.zeros((blk - SROWS, HD_SB), F32)

    def blocks(first):
        mx = None
        for h in range(n_heads):
            cols = slice(h * HD_SB, (h + 1) * HD_SB)
            rows = slice(h * SROWS, (h + 1) * SROWS)
            qb = q_ref[0, :, cols].astype(BF16)
            if first:
                kblk = jnp.concatenate([kn_ref[0, :, cols], pad], axis=0).astype(BF16)
                vblk = jnp.concatenate([vn_ref[0, :, cols], pad], axis=0).astype(BF16)
                carry0 = jnp.zeros((SROWS, blk), F32)
            else:
                kblk = kbuf[pl.ds(h, PAGE_SIZE, stride=n_heads), :].astype(BF16)
                vblk = vbuf[pl.ds(h, PAGE_SIZE, stride=n_heads), :].astype(BF16)
                carry0 = carry_scr[rows, :]
            carry, o = _sb_block(qb, kblk, vblk, uu, carry0, strict if first else None, scale)
            carry_scr[rows, :] = carry
            acc_scr[rows, :] = o if first else acc_scr[rows, :] + o
            top = jnp.max(jnp.where(real_row, carry, -jnp.inf))
            mx = top if mx is None else jnp.maximum(mx, top)
        return mx

    def cond(st):
        p, mx = st
        return jnp.logical_and(p >= 0, mx > SB_EXIT)

    def body(st):
        p, _ = st
        page = pt_ref[b, p]
        ck = pltpu.make_async_copy(kc_hbm.at[layer, page], kbuf, sem.at[0])
        cv = pltpu.make_async_copy(vc_hbm.at[layer, page], vbuf, sem.at[1])
        ck.start()
        cv.start()
        ck.wait()
        cv.wait()
        return p - 1, blocks(False)

    lax.while_loop(cond, body, (n_pages - 1, blocks(True)))
    for h in range(n_heads):
        o_ref[:, h * HD_SB:(h + 1) * HD_SB] = acc_scr[h * SROWS:(h + 1) * SROWS, :]


def _sb_sample(proj_s, cache_k, cache_v, e, page_table, uu):
    _, sr, width = proj_s.shape
    nh = width // HD_SB
    nreq, n_pages = page_table.shape
    body = functools.partial(_sb_sample_body, layer=e, n_heads=nh, n_pages=n_pages,
                             scale=HD_SB ** -0.5)
    spec = lambda g: pl.BlockSpec((1, SROWS, width), lambda b, pt: (g, b, 0))
    return pl.pallas_call(
        body,
        grid_spec=pltpu.PrefetchScalarGridSpec(
            num_scalar_prefetch=1,
            grid=(nreq,),
            in_specs=[spec(0), spec(1), spec(2),
                      pl.BlockSpec(memory_space=pl.ANY),
                      pl.BlockSpec(memory_space=pl.ANY),
                      pl.BlockSpec((2 * ATT_BLK, 2 * ATT_BLK), lambda b, pt: (0, 0))],
            out_specs=pl.BlockSpec((SROWS, width), lambda b, pt: (b, 0)),
            scratch_shapes=[pltpu.VMEM((PAGE_SIZE * nh, HD_SB), F32),
                            pltpu.VMEM((PAGE_SIZE * nh, HD_SB), F32),
                            pltpu.SemaphoreType.DMA((2,)),
                            pltpu.VMEM((nh * SROWS, ATT_BLK), F32),
                            pltpu.VMEM((nh * SROWS, HD_SB), F32)],
        ),
        out_shape=jax.ShapeDtypeStruct((sr, width), F32),
        compiler_params=_cp(("arbitrary",)),
        name="sb_sample",
    )(page_table, proj_s, proj_s, proj_s, cache_k, cache_v, uu)


def _ret_tables(pos, chunk, n_heads):
    half = DK_RET // 2
    inv = np.float32(RET_THETA) ** (-np.arange(half, dtype=np.float32) / half)
    ang = pos.astype(F32)[:, None] * jnp.asarray(inv)[None, :]
    cos = jnp.concatenate([jnp.cos(ang), jnp.cos(ang)], axis=1)
    sin = jnp.concatenate([-jnp.sin(ang), jnp.sin(ang)], axis=1)
    log_g = jnp.log1p(-(2.0 ** (-5.0 - jnp.arange(n_heads, dtype=F32))))
    i = jnp.arange(chunk, dtype=F32)
    return cos, sin, log_g, i


def _rope_half(x, cos, sin):
    return x * cos + pltpu.roll(x, x.shape[-1] // 2, axis=1) * sin


def _head_ln_gate(o, gain, g):
    mu = jnp.mean(o, axis=-1, keepdims=True)
    d = o - mu
    var = jnp.mean(d * d, axis=-1, keepdims=True)
    return d * lax.rsqrt(var + EPS) * gain * (g * jax.nn.sigmoid(g))


def _ret_prompt_body(q_ref, k_ref, v_ref, g_ref, cos_ref, sin_ref, dec_ref, qd_ref, kd_ref,
                     gc_ref, gain_ref, o_ref, s_ref, s_scr, *, n_chunks):
    c = ATT_BLK
    s_scr[...] = jnp.zeros_like(s_scr)
    decay = dec_ref[0]
    qd = qd_ref[0]
    kd = kd_ref[0]
    gc = gc_ref[0, 0:1, :1]
    gain = gain_ref[...]

    def body(n, carry):
        rows = pl.ds(pl.multiple_of(n * c, c), c)
        cos = cos_ref[rows, :]
        sin = sin_ref[rows, :]
        q = _rope_half(q_ref[0, rows, :], cos, sin)
        k = _rope_half(k_ref[0, rows, :], cos, sin) * (DK_RET ** -0.5)
        vb = v_ref[0, rows, :].astype(BF16)
        s_prev = s_scr[...]
        scores = _dot_nt(q.astype(BF16), k.astype(BF16)) * decay
        o = _dot(scores.astype(BF16), vb) + _dot((q * qd).astype(BF16), s_prev.astype(BF16))
        kv = _dot(jnp.transpose(k * kd).astype(BF16), vb)
        s_scr[...] = gc * s_prev + kv
        o_ref[rows, :] = _head_ln_gate(o, gain, g_ref[0, rows, :]).astype(o_ref.dtype)
        return carry

    lax.fori_loop(0, n_chunks, body, 0)
    s_ref[0, 0] = s_scr[...]


def _ret_prompt(proj, gain):
    _, tp, width = proj.shape
    nb = tp // SEQ
    nh = width // DV_RET
    c = ATT_BLK
    cos, sin, log_g, i = _ret_tables(jnp.arange(SEQ), c, nh)
    diff = i[:, None] - i[None, :]
    decay = jnp.where(diff >= 0, jnp.exp(jnp.maximum(diff, 0.0)[None] * log_g[:, None, None]), 0.0)
    ones = jnp.ones((1, 1, DK_RET), F32)
    qd = jnp.exp((i + 1)[None, :, None] * log_g[:, None, None]) * ones
    kd = jnp.exp((c - 1 - i)[None, :, None] * log_g[:, None, None]) * ones
    gc = jnp.exp(c * log_g)[:, None, None] * jnp.ones((1, SUBLANES, LANES), F32)
    body = functools.partial(_ret_prompt_body, n_chunks=SEQ // c)
    tab = lambda: pl.BlockSpec((SEQ, DK_RET), lambda b, h: (0, 0))
    per_head = lambda r, w: pl.BlockSpec((1, r, w), lambda b, h: (h, 0, 0))
    return pl.pallas_call(
        body,
        grid=(nb, nh),
        in_specs=[
            pl.BlockSpec((1, SEQ, DK_RET), lambda b, h: (3, b, h)),
            pl.BlockSpec((1, SEQ, DK_RET), lambda b, h: (3, b, nh + h)),
            pl.BlockSpec((1, SEQ, DV_RET), lambda b, h: (4, b, h)),
            pl.BlockSpec((1, SEQ, DV_RET), lambda b, h: (5, b, h)),
            tab(), tab(),
            per_head(c, c), per_head(c, DK_RET), per_head(c, DK_RET), per_head(SUBLANES, LANES),
            pl.BlockSpec((1, DV_RET), lambda b, h: (0, h)),
        ],
        out_specs=[pl.BlockSpec((SEQ, DV_RET), lambda b, h: (b, h)),
                   pl.BlockSpec((1, 1, DK_RET, DV_RET), lambda b, h: (b, h, 0, 0))],
        out_shape=[jax.ShapeDtypeStruct((tp, width), BF16),
                   jax.ShapeDtypeStruct((nb, nh, DK_RET, DV_RET), F32)],
        scratch_shapes=[pltpu.VMEM((DK_RET, DV_RET), F32)],
        compiler_params=_cp(("arbitrary", "arbitrary")),
        name="ret_prompt",
    )(proj, proj, proj, proj, cos, sin, decay, qd, kd, gc, gain)


def _ret_sample_body(q_ref, k_ref, v_ref, g_ref, cos_ref, sin_ref, dec_ref, qd_ref, kd_ref,
                     gc_ref, gain_ref, s0_ref, o_ref, s_ref, *, n_req):
    sr = q_ref.shape[1]
    cos = cos_ref[...]
    sin = sin_ref[...]
    q = _rope_half(q_ref[0], cos, sin)
    k = _rope_half(k_ref[0], cos, sin) * (DK_RET ** -0.5)
    v = v_ref[0]
    vb = v.astype(BF16)
    scores = _dot_nt(q.astype(BF16), k.astype(BF16)) * dec_ref[0]
    o = _dot(scores.astype(BF16), vb)
    qdec = q * qd_ref[0]
    kdec = k * kd_ref[0]
    gc = gc_ref[0, 0:1, :1]
    row = lax.broadcasted_iota(jnp.int32, (sr, 1), 0)
    for b in range(n_req):
        mine = jnp.logical_and(row >= b * SROWS, row < (b + 1) * SROWS)
        s0 = s0_ref[b, 0]
        o = o + _dot(jnp.where(mine, qdec, 0.0).astype(BF16), s0.astype(BF16))
        kb = jnp.transpose(jnp.where(mine, kdec, 0.0)).astype(BF16)
        s_ref[b, 0] = gc * s0 + _dot(kb, vb)
    o_ref[...] = _head_ln_gate(o, gain_ref[...], g_ref[0])


def _ret_sample(proj_s, gain, state0, past_len):
    _, sr, width = proj_s.shape
    nh = width // DV_RET
    nreq = sr // SROWS
    t = jnp.arange(sr) % SROWS
    cos, sin, log_g, _ = _ret_tables(past_len + t, DEC_SEQ, nh)
    tf = t.astype(F32)
    diff = tf[:, None] - tf[None, :]
    same = (jnp.arange(sr)[:, None] // SROWS) == (jnp.arange(sr)[None, :] // SROWS)
    real = (t < DEC_SEQ)
    ok = same & (diff >= 0) & real[:, None] & real[None, :]
    decay = jnp.where(ok[None], jnp.exp(jnp.maximum(diff, 0.0)[None] * log_g[:, None, None]), 0.0)
    ones = jnp.ones((1, 1, DK_RET), F32)
    qd = jnp.exp((tf + 1)[None, :, None] * log_g[:, None, None]) * ones
    kd = jnp.where(real[None, :, None],
                   jnp.exp((DEC_SEQ - 1 - tf)[None, :, None] * log_g[:, None, None]), 0.0) * ones
    gc = jnp.exp(DEC_SEQ * log_g)[:, None, None] * jnp.ones((1, SUBLANES, LANES), F32)
    body = functools.partial(_ret_sample_body, n_req=nreq)
    tab = lambda: pl.BlockSpec((sr, DK_RET), lambda h: (0, 0))
    per_head = lambda r, w: pl.BlockSpec((1, r, w), lambda h: (h, 0, 0))
    st = pl.BlockSpec((nreq, 1, DK_RET, DV_RET), lambda h: (0, h, 0, 0))
    return pl.pallas_call(
        body,
        grid=(nh,),
        in_specs=[
            pl.BlockSpec((1, sr, DK_RET), lambda h: (3, 0, h)),
            pl.BlockSpec((1, sr, DK_RET), lambda h: (3, 0, nh + h)),
            pl.BlockSpec((1, sr, DV_RET), lambda h: (4, 0, h)),
            pl.BlockSpec((1, sr, DV_RET), lambda h: (5, 0, h)),
            tab(), tab(),
            per_head(sr, sr), per_head(sr, DK_RET), per_head(sr, DK_RET), per_head(SUBLANES, LANES),
            pl.BlockSpec((1, DV_RET), lambda h: (0, h)),
            st,
        ],
        out_specs=[pl.BlockSpec((sr, DV_RET), lambda h: (0, h)), st],
        out_shape=[jax.ShapeDtypeStruct((sr, width), F32),
                   jax.ShapeDtypeStruct(state0.shape, F32)],
        compiler_params=_cp(("arbitrary",)),
        name="ret_sample",
    )(proj_s, proj_s, proj_s, proj_s, cos, sin, decay, qd, kd, gc, gain, state0)


def _diff_rope_tables(pos):
    half = ROT_DIM // 2
    inv = np.float32(ROPE_THETA) ** (-np.arange(half, dtype=np.float32) / half)
    ang = pos.astype(F32)[:, None] * jnp.asarray(inv)[None, :]
    cos, sin = jnp.cos(ang), jnp.sin(ang)
    n = pos.shape[0]
    rest = HD_DIFF - ROT_DIM
    c = jnp.concatenate([cos, cos, jnp.ones((n, rest), F32)], axis=1)
    s1 = jnp.concatenate([-sin, jnp.zeros((n, half + rest), F32)], axis=1)
    s2 = jnp.concatenate([jnp.zeros((n, half), F32), sin, jnp.zeros((n, rest), F32)], axis=1)
    two = lambda a: jnp.concatenate([a, a], axis=1)
    return two(c), two(s1), two(s2)


def _seg_mean_mat():
    a = np.arange(2 * HD_DIFF)
    return jnp.asarray((a[:, None] // HD_DIFF == a[None, :] // HD_DIFF).astype(np.float32) / HD_DIFF,
                       dtype=BF16)


def _qk_norm_rope(x, gain, c, s1, s2, seg):
    hi, lo = _split_bf16(x * x)
    ms = _dot(hi, seg) + _dot(lo, seg)
    y = x * lax.rsqrt(ms + EPS) * gain
    half = ROT_DIM // 2
    return y * c + pltpu.roll(y, LANES - half, axis=1) * s1 + pltpu.roll(y, half, axis=1) * s2


def _diff_prep_body(q_ref, k_ref, qg_ref, kg_ref, c_ref, s1_ref, s2_ref, seg_ref, qo_ref, ko_ref,
                    *, n_heads, q_scale):
    c, s1, s2, seg = c_ref[...], s1_ref[...], s2_ref[...], seg_ref[...]
    for h in range(n_heads):
        cols = slice(h * LANES, (h + 1) * LANES)
        q = _qk_norm_rope(q_ref[0, :, cols], qg_ref[...], c, s1, s2, seg)
        qo_ref[:, cols] = (q * q_scale).astype(qo_ref.dtype)
        ko_ref[:, cols] = _qk_norm_rope(k_ref[0, :, cols], kg_ref[...], c, s1, s2, seg)


def _diff_prep(proj, qk_gain, pos, rows_per_pos_table, q_dtype):
    _, rows, width = proj.shape
    nh = width // LANES
    tm = min(256, rows)
    c, s1, s2 = _diff_rope_tables(pos)
    nt = rows_per_pos_table // tm
    qg = jnp.tile(qk_gain[0], 2)[None, :]
    kg = jnp.tile(qk_gain[1], 2)[None, :]
    body = functools.partial(_diff_prep_body, n_heads=nh, q_scale=HD_DIFF ** -0.5 * math.log2(math.e))
    tab = lambda: pl.BlockSpec((tm, LANES), lambda i: (i % nt, 0))
    vec = lambda: pl.BlockSpec((1, LANES), lambda i: (0, 0))
    return pl.pallas_call(
        body,
        grid=(rows // tm,),
        in_specs=[pl.BlockSpec((1, tm, width), lambda i: (0, i, 0)),
                  pl.BlockSpec((1, tm, width), lambda i: (1, i, 0)),
                  vec(), vec(), tab(), tab(), tab(),
                  pl.BlockSpec((LANES, LANES), lambda i: (0, 0))],
        out_specs=[pl.BlockSpec((tm, width), lambda i: (i, 0)),
                   pl.BlockSpec((tm, width), lambda i: (i, 0))],
        out_shape=[jax.ShapeDtypeStruct((rows, width), q_dtype),
                   jax.ShapeDtypeStruct((rows, width), F32)],
        compiler_params=_cp(("arbitrary",)),
        name="diff_prep",
    )(proj, proj, qg, kg, c, s1, s2, _seg_mean_mat())


def _diff_lambda(lam_ref, lam_init):
    lv = lam_ref[...]
    a = jnp.sum(lv[0:1] * lv[1:2], axis=-1, keepdims=True)
    b = jnp.sum(lv[2:3] * lv[3:4], axis=-1, keepdims=True)
    return jnp.exp(a) - jnp.exp(b) + lam_init


def _stack_components(q):
    lane = lax.broadcasted_iota(jnp.int32, q.shape, 1)
    zero = jnp.zeros_like(q)
    return jnp.concatenate([jnp.where(lane < HD_DIFF, q, zero), jnp.where(lane >= HD_DIFF, q, zero)],
                           axis=0)


def _softmax_step(qz, kblk, vblk, m, l, acc, mask):
    s = _dot_nt(qz, kblk)
    if mask is not None:
        s = jnp.where(mask, s, -jnp.inf)
    m_new = jnp.maximum(m, jnp.max(s, axis=-1, keepdims=True))
    alpha = jnp.exp2(m - m_new)
    p = jnp.exp2(s - m_new)
    l = alpha * l + jnp.sum(p, axis=-1, keepdims=True)
    acc = alpha * acc + _dot(p.astype(BF16), vblk)
    return m_new, l, acc


def _diff_finish(l, acc, n, lam, sub_g, out_scale):
    o = acc[:n] / l[:n] - lam * (acc[n:] / l[n:])
    y = o * lax.rsqrt(jnp.mean(o * o, axis=-1, keepdims=True) + EPS) * sub_g
    return y * out_scale


def _diff_prompt_body(q_ref, k_ref, v_ref, lam_ref, subg_ref, o_ref, kb_scr, vt_scr, qt_scr, qz_scr,
                      m_scr, l_scr, acc_scr, *, nq, tile, n_heads, lam_init):
    kb_scr[...] = k_ref[...].astype(BF16)
    head_cols = [slice(h * DV_DIFF, (h + 1) * DV_DIFF) for h in range(n_heads)]
    for h, cols in enumerate(head_cols):
        for j in range(nq):
            blk = slice(j * tile, (j + 1) * tile)
            vt_scr[h, j] = jnp.transpose(v_ref[0, blk, cols]).astype(BF16)
            qt_scr[h, j] = jnp.transpose(q_ref[blk, cols].astype(F32)).astype(BF16)
    lam = _diff_lambda(lam_ref, lam_init)
    key = lax.broadcasted_iota(jnp.int32, (tile, 2 * tile), 0)
    qry = lax.broadcasted_iota(jnp.int32, (tile, 2 * tile), 1) & (tile - 1)
    causal = key <= qry
    first_comp = lax.broadcasted_iota(jnp.int32, (DV_DIFF, tile), 0) < HD_DIFF

    def key_rows(kb):
        return pl.ds(pl.multiple_of(kb * tile, tile), tile)

    def steps(kbs, mask):
        for h, cols in enumerate(head_cols):
            scores = [_dot(kb_scr[key_rows(kb), cols], qz_scr[h]) for kb in kbs]
            if mask is not None:
                scores[-1] = jnp.where(mask, scores[-1], -jnp.inf)
            m_old = m_scr[h]
            m_new = m_old
            for s in scores:
                m_new = jnp.maximum(m_new, jnp.max(s, axis=0, keepdims=True))
            alpha = jnp.exp2(m_old - m_new)
            l = alpha * l_scr[h]
            acc = alpha * acc_scr[h]
            for kb, s in zip(kbs, scores):
                p = jnp.exp2(s - m_new)
                l = l + jnp.sum(p, axis=0, keepdims=True)
                acc = acc + _dot(vt_scr[h, kb], p.astype(BF16))
            l_scr[h] = l
            acc_scr[h] = acc
            m_scr[h] = m_new

    def q_body(qi, c):
        rows = pl.ds(pl.multiple_of(qi * tile, tile), tile)
        for h in range(n_heads):
            qt = qt_scr[h, qi]
            zero = jnp.zeros_like(qt)
            qz_scr[h] = jnp.concatenate([jnp.where(first_comp, qt, zero),
                                         jnp.where(first_comp, zero, qt)], axis=1)
        m_scr[...] = jnp.full(m_scr.shape, -jnp.inf, F32)
        l_scr[...] = jnp.zeros_like(l_scr)
        acc_scr[...] = jnp.zeros_like(acc_scr)

        def kv_body(j, c2):
            steps([2 * j, 2 * j + 1], None)
            return c2

        lax.fori_loop(0, lax.shift_right_logical(qi, 1), kv_body, 0)

        @pl.when((qi & 1) == 1)
        def _():
            steps([qi - 1, qi], causal)

        @pl.when((qi & 1) == 0)
        def _():
            steps([qi], causal)
        for h, cols in enumerate(head_cols):
            ot = acc_scr[h] / l_scr[h]
            d = ot[:, :tile] - lam * ot[:, tile:]
            y = d * lax.rsqrt(jnp.mean(d * d, axis=0, keepdims=True) + EPS) * subg_ref[...]
            o_ref[rows, cols] = jnp.transpose(y * (1.0 - lam_init)).astype(o_ref.dtype)
        return c

    lax.fori_loop(0, nq, q_body, 0)


def _diff_prompt(qn, kn, proj, diff_lambda, sub_g, lam_init):
    tp, width = qn.shape
    nb = tp // SEQ
    nh = width // DV_DIFF
    hpb = min(4, nh)
    tile = min(256, SEQ)
    w = hpb * DV_DIFF
    body = functools.partial(_diff_prompt_body, nq=SEQ // tile, tile=tile, n_heads=hpb,
                             lam_init=lam_init)
    return pl.pallas_call(
        body,
        grid=(nb, nh // hpb),
        in_specs=[pl.BlockSpec((SEQ, w), lambda b, h: (b, h)),
                  pl.BlockSpec((SEQ, w), lambda b, h: (b, h)),
                  pl.BlockSpec((1, SEQ, w), lambda b, h: (2, b, h)),
                  pl.BlockSpec(diff_lambda.shape, lambda b, h: (0, 0)),
                  pl.BlockSpec((DV_DIFF, 1), lambda b, h: (0, 0))],
        out_specs=pl.BlockSpec((SEQ, w), lambda b, h: (b, h)),
        out_shape=jax.ShapeDtypeStruct((tp, width), BF16),
        scratch_shapes=[pltpu.VMEM((SEQ, w), BF16),
                        pltpu.VMEM((hpb, SEQ // tile, DV_DIFF, tile), BF16),
                        pltpu.VMEM((hpb, SEQ // tile, DV_DIFF, tile), BF16),
                        pltpu.VMEM((hpb, DV_DIFF, 2 * tile), BF16),
                        pltpu.VMEM((hpb, 1, 2 * tile), F32), pltpu.VMEM((hpb, 1, 2 * tile), F32),
                        pltpu.VMEM((hpb, DV_DIFF, 2 * tile), F32)],
        compiler_params=_cp(("arbitrary", "arbitrary")),
        name="diff_prompt",
    )(qn, kn, proj, diff_lambda, sub_g[:, None])


def _diff_sample_body(pt_ref, q_ref, kn_ref, vn_ref, bias_ref, *rest, n_heads, n_steps, pages_per_step,
                      lam_init):
    kc_refs = rest[:pages_per_step]
    vc_refs = rest[pages_per_step:2 * pages_per_step]
    lam_ref, subg_ref, o_ref, m_scr, l_scr, acc_scr = rest[2 * pages_per_step:]
    s = pl.program_id(1)
    nq = 2 * DEC_SEQ

    @pl.when(s == 0)
    def _():
        m_scr[...] = jnp.full(m_scr.shape, -jnp.inf, F32)
        l_scr[...] = jnp.zeros_like(l_scr)
        acc_scr[...] = jnp.zeros_like(acc_scr)

    qz = q_ref[0]
    scores = [_dot_nt(qz, kc_ref[0, 0].astype(BF16)) + bias_ref[...] for kc_ref in kc_refs]
    m_old = m_scr[...]
    m_new = m_old
    for sc in scores:
        m_new = jnp.maximum(m_new, jnp.max(sc, axis=-1, keepdims=True))
    alpha = jnp.exp2(m_old - m_new)
    l = alpha * l_scr[...]
    acc = alpha * acc_scr[...]
    for sc, vc_ref in zip(scores, vc_refs):
        p = jnp.exp2(sc - m_new)
        l = l + jnp.sum(p, axis=-1, keepdims=True)
        acc = acc + _dot(p.astype(BF16), vc_ref[0, 0].astype(BF16))
    l_scr[...] = l
    acc_scr[...] = acc
    m_scr[...] = m_new

    @pl.when(s == n_steps - 1)
    def _():
        lam = _diff_lambda(lam_ref, lam_init)
        row = lax.broadcasted_iota(jnp.int32, (nq, ATT_BLK), 0) & (DEC_SEQ - 1)
        col = lax.broadcasted_iota(jnp.int32, (nq, ATT_BLK), 1)
        causal = col <= row
        pad = jnp.zeros((ATT_BLK - SROWS, LANES), F32)
        first_comp = lax.broadcasted_iota(jnp.int32, (nq, DV_DIFF), 0) < DEC_SEQ
        qf = qz.astype(F32)
        for h in range(n_heads):
            cols = slice(h * LANES, (h + 1) * LANES)
            rows = slice(h * nq, (h + 1) * nq)
            kblk = jnp.concatenate([kn_ref[:, cols], pad], axis=0).astype(BF16)
            vblk = jnp.concatenate([vn_ref[0, :, cols], pad], axis=0).astype(BF16)
            _, l, acc = _softmax_step(qf[rows].astype(BF16), kblk, vblk, m_scr[rows, :],
                                      l_scr[rows, :], acc_scr[rows, :], causal)
            o2 = acc / l
            o = o2 - lam * pltpu.roll(o2, DEC_SEQ, axis=0)
            y = o * lax.rsqrt(jnp.mean(o * o, axis=-1, keepdims=True) + EPS) * subg_ref[...]
            o_ref[:, cols] = jnp.where(first_comp, y * (1.0 - lam_init), 0.0)


def _diff_sample(qn_s, kn_s, proj_s, cache_k, cache_v, o, page_table, diff_lambda, sub_g, lam_init):
    sr, width = qn_s.shape
    nh = width // DV_DIFF
    nreq, n_pages = page_table.shape
    assert 2 * DEC_SEQ == SROWS
    pps = 4 if n_pages % 4 == 0 else 1
    n_steps = n_pages // pps
    nq = 2 * DEC_SEQ
    rows = nh * nq
    q4 = jnp.transpose(qn_s.reshape(nreq, SROWS, nh, DV_DIFF)[:, :DEC_SEQ], (0, 2, 1, 3))
    lane = jnp.arange(DV_DIFF)
    qz = jnp.stack([jnp.where(lane < HD_DIFF, q4, 0.0), jnp.where(lane >= HD_DIFF, q4, 0.0)], axis=2)
    qz = qz.reshape(nreq, rows, DV_DIFF).astype(BF16)
    same_head = (np.arange(PAGE_SIZE * nh)[None, :] % nh) == (np.arange(rows)[:, None] // nq)
    bias = jnp.asarray(np.where(same_head, 0.0, -np.inf), F32)
    body = functools.partial(_diff_sample_body, n_heads=nh, n_steps=n_steps, pages_per_step=pps,
                             lam_init=lam_init)
    page = lambda g: pl.BlockSpec((1, 1, PAGE_SIZE * nh, DV_DIFF),
                                  lambda b, s, pt: (o, pt[b, s * pps + g], 0, 0))
    return pl.pallas_call(
        body,
        grid_spec=pltpu.PrefetchScalarGridSpec(
            num_scalar_prefetch=1,
            grid=(nreq, n_steps),
            in_specs=[pl.BlockSpec((1, rows, DV_DIFF), lambda b, s, pt: (b, 0, 0)),
                      pl.BlockSpec((SROWS, width), lambda b, s, pt: (b, 0)),
                      pl.BlockSpec((1, SROWS, width), lambda b, s, pt: (2, b, 0)),
                      pl.BlockSpec(bias.shape, lambda b, s, pt: (0, 0))]
                     + [page(g) for g in range(pps)] + [page(g) for g in range(pps)]
                     + [pl.BlockSpec(diff_lambda.shape, lambda b, s, pt: (0, 0)),
                        pl.BlockSpec((1, DV_DIFF), lambda b, s, pt: (0, 0))],
            out_specs=pl.BlockSpec((SROWS, width), lambda b, s, pt: (b, 0)),
            scratch_shapes=[pltpu.VMEM((rows, 1), F32), pltpu.VMEM((rows, 1), F32),
                            pltpu.VMEM((rows, DV_DIFF), F32)],
        ),
        out_shape=jax.ShapeDtypeStruct((sr, width), F32),
        compiler_params=_cp(("arbitrary", "arbitrary")),
        name="diff_sample",
    )(page_table, qz, kn_s, proj_s, bias, *([cache_k] * pps), *([cache_v] * pps),
      diff_lambda, sub_g[None, :])


def _route(h, whi_ref, wlo_ref, b_ref, ltri_ref, cnt_scr, row_real):
    n = h.shape[0]
    hi, lo = _split_bf16(h)
    logits = _dot(hi, whi_ref[...]) + _dot(lo, whi_ref[...]) + _dot(hi, wlo_ref[...]) + b_ref[...]
    lane = lax.broadcasted_iota(jnp.int32, logits.shape, 1).astype(F32)
    neg = -jnp.inf
    first = lambda hit: jnp.min(jnp.where(hit, lane, float(LANES)), axis=-1, keepdims=True)
    gl = jnp.where(lane < N_GROUPS, logits, neg)
    gmax = jnp.max(gl, axis=-1, keepdims=True)
    g_top = 1.0 / jnp.sum(jnp.exp(gl - gmax), axis=-1, keepdims=True)
    g_idx = first(gl == gmax)
    lo_lane = N_GROUPS + EXP_PER_GROUP * g_idx
    el = jnp.where(jnp.logical_and(lane >= lo_lane, lane < lo_lane + EXP_PER_GROUP), logits, neg)
    emax = jnp.max(el, axis=-1, keepdims=True)
    esum = jnp.sum(jnp.exp(el - emax), axis=-1, keepdims=True)
    l1 = first(el == emax)
    el2 = jnp.where(lane == l1, neg, el)
    e2max = jnp.max(el2, axis=-1, keepdims=True)
    l2 = first(el2 == e2max)
    p1 = 1.0 / esum
    p2 = jnp.exp(e2max - emax) / esum
    w1 = p1 / (p1 + p2) * g_top
    w2 = p2 / (p1 + p2) * g_top
    e1 = l1 - N_GROUPS
    e2 = l2 - N_GROUPS
    hit1 = lane == e1
    hit2 = lane == e2
    both = jnp.logical_or(hit1, hit2)
    if row_real is not None:
        both = jnp.logical_and(both, row_real)
    both = jnp.where(both, 1.0, 0.0)
    before = _dot(ltri_ref[:n, :n], both.astype(BF16)) + cnt_scr[...]
    rank1 = jnp.sum(jnp.where(hit1, before, 0.0), axis=-1, keepdims=True)
    rank2 = jnp.sum(jnp.where(hit2, before, 0.0), axis=-1, keepdims=True)
    cnt_scr[...] += jnp.sum(both, axis=0, keepdims=True)
    rec = jnp.where(lane == 0, e1, jnp.where(lane == 1, e2, jnp.where(lane == 2, w1, jnp.where(
        lane == 3, w2, jnp.where(lane == 4, rank1, jnp.where(lane == 5, rank2, 0.0))))))
    return hi, rec


def _router_body(xp_ref, xs_ref, g_ref, sh_ref, sc_ref, shs_ref, scs_ref, whi_ref, wlo_ref, b_ref,
                 ltri_ref, h_ref, rec_ref, cnt_ref, cnt_scr, *, n_p, tiles_per_batch):
    i = pl.program_id(0)
    tm = xp_ref.shape[0]
    sr = xs_ref.shape[0]

    @pl.when(i == 0)
    def _():
        cnt_scr[...] = jnp.zeros_like(cnt_scr)

    @pl.when(i < n_p)
    def _():
        b = i // tiles_per_batch
        sh = sh_ref[0, 0, pl.ds(b, 1), :]
        sc = sc_ref[0, 0, pl.ds(b, 1), :]
        chunk = min(256, tm)

        def body(r, c):
            sl = pl.ds(pl.multiple_of(r * chunk, chunk), chunk)
            hi, rec = _route(_modulate(xp_ref[sl, :], g_ref[...], sh, sc), whi_ref, wlo_ref, b_ref,
                             ltri_ref, cnt_scr, None)
            h_ref[sl, :] = hi
            rec_ref[sl, :] = rec
            return c

        lax.fori_loop(0, tm // chunk, body, 0)

    @pl.when(i == n_p)
    def _():
        hi, rec = _route(_mod_sample_tile(xs_ref, g_ref[...], shs_ref, scs_ref), whi_ref, wlo_ref, b_ref,
                         ltri_ref, cnt_scr, _sample_row_valid(sr))
        h_ref[:sr, :] = hi
        rec_ref[:sr, :] = rec
        h_ref[sr:, :] = jnp.zeros((tm - sr, h_ref.shape[1]), h_ref.dtype)
        rec_ref[sr:, :] = jnp.zeros((tm - sr, LANES), F32)

    cnt_ref[...] = jnp.broadcast_to(cnt_scr[...], cnt_ref.shape)


def _router(xp, xs, g, mods, li, mod_s, w_rg, b_rg, w_re, b_re):
    tp, d = xp.shape
    sr = xs.shape[0]
    tm = _prompt_tile()
    n_p = tp // tm
    last = n_p - 1
    wr = jnp.concatenate([w_rg, jnp.moveaxis(w_re, 0, 1).reshape(d, N_EXPERTS),
                          jnp.zeros((d, LANES - N_GROUPS - N_EXPERTS), F32)], axis=1)
    whi = wr.astype(BF16)
    wlo = (wr - whi.astype(F32)).astype(BF16)
    bias = jnp.concatenate([b_rg, b_re.reshape(-1), jnp.zeros((LANES - N_GROUPS - N_EXPERTS,), F32)])[None]
    body = functools.partial(_router_body, n_p=n_p, tiles_per_batch=SEQ // tm)
    full = lambda r, c: pl.BlockSpec((r, c), lambda i: (0, 0))
    rows = tp + tm
    chunk = min(256, tm)
    ltri = jnp.asarray(np.tril(np.ones((chunk, chunk), np.float32), -1), dtype=BF16)
    return pl.pallas_call(
        body,
        grid=(rows // tm,),
        in_specs=[pl.BlockSpec((tm, d), lambda i: (jnp.minimum(i, last), 0)),
                  full(sr, d),
                  pl.BlockSpec((1, d), lambda i: (0, 0)),
                  pl.BlockSpec((1, 1, MOD_ROWS, d), lambda i: (li, 3, 0, 0)),
                  pl.BlockSpec((1, 1, MOD_ROWS, d), lambda i: (li, 4, 0, 0)),
                  full(sr, d), full(sr, d), full(d, LANES), full(d, LANES), full(1, LANES),
                  full(chunk, chunk)],
        out_specs=[pl.BlockSpec((tm, d), lambda i: (i, 0)),
                   pl.BlockSpec((tm, LANES), lambda i: (i, 0)),
                   full(SUBLANES, LANES)],
        out_shape=[jax.ShapeDtypeStruct((rows, d), BF16),
                   jax.ShapeDtypeStruct((rows, LANES), F32),
                   jax.ShapeDtypeStruct((SUBLANES, LANES), F32)],
        scratch_shapes=[pltpu.VMEM((1, LANES), F32)],
        compiler_params=_cp(("arbitrary",)),
        name="router",
    )(xp, xs, g, mods, mods, mod_s[3], mod_s[4], whi, wlo, bias, ltri)


def _experts_body(te_ref, nv_ref, *refs, starts):
    x_refs = refs[:len(starts)]
    wi_ref, wo_ref, y_ref, wi_scr, wo_scr = refs[len(starts):]
    t = pl.program_id(0)
    f = wo_ref.shape[2]
    fresh = jnp.logical_or(t == 0, te_ref[t] != te_ref[jnp.maximum(t - 1, 0)])

    @pl.when(jnp.logical_and(t < nv_ref[0], fresh))
    def _():
        wi_scr[...] = wi_ref[0, 0].astype(BF16)
        wo_scr[...] = wo_ref[0, 0].astype(BF16)

    @pl.when(t < nv_ref[0])
    def _():
        x = x_refs[0][...]
        for start, x_ref in zip(starts[1:], x_refs[1:]):
            x = jnp.where(t >= start, x_ref[...], x)
        hid = _dot(x, wi_scr[...])
        gate = hid[:, f:]
        act = (gate * jax.nn.sigmoid(gate)) * hid[:, :f]
        y_ref[...] = _dot(act.astype(BF16), wo_scr[...]).astype(y_ref.dtype)

    @pl.when(t >= nv_ref[0])
    def _():
        y_ref[...] = jnp.zeros_like(y_ref)


def _experts(x_parts, tile_expert, n_valid, w_in, w_out, li):
    d = x_parts[0].shape[1]
    sizes = [x.shape[0] // EXP_TILE for x in x_parts]
    starts = tuple(int(s) for s in np.cumsum([0] + sizes[:-1]))
    p = sum(sizes) * EXP_TILE
    f2 = w_in.shape[3]

    def part(start, size):
        return pl.BlockSpec((EXP_TILE, d), lambda t, te, nv: (jnp.clip(t - start, 0, size - 1), 0))

    return pl.pallas_call(
        functools.partial(_experts_body, starts=starts),
        grid_spec=pltpu.PrefetchScalarGridSpec(
            num_scalar_prefetch=2,
            grid=(p // EXP_TILE,),
            in_specs=[part(s, n) for s, n in zip(starts, sizes)]
                     + [pl.BlockSpec((1, 1, d, f2), lambda t, te, nv: (li, te[t], 0, 0)),
                      pl.BlockSpec((1, 1, f2 // 2, d), lambda t, te, nv: (li, te[t], 0, 0))],
            out_specs=pl.BlockSpec((EXP_TILE, d), lambda t, te, nv: (t, 0)),
            scratch_shapes=[pltpu.VMEM((d, f2), BF16), pltpu.VMEM((f2 // 2, d), BF16)],
        ),
        out_shape=jax.ShapeDtypeStruct((p, d), BF16),
        compiler_params=_cp(("arbitrary",)),
        name="experts",
    )(tile_expert, n_valid, *x_parts, w_in, w_out)


def _combine_body(y1p_ref, y2p_ref, y1s_ref, y2s_ref, recp_ref, recs_ref, xp_ref, xs_ref,
                  gate_ref, gates_ref, op_ref, os_ref, *, n_p, tiles_per_batch):
    i = pl.program_id(0)

    def mix(y1, y2, rec):
        return rec[:, 2:3] * y1.astype(F32) + rec[:, 3:4] * y2.astype(F32)

    @pl.when(i < n_p)
    def _():
        b = i // tiles_per_batch
        gate = gate_ref[0, 0, pl.ds(b, 1), :]
        op_ref[...] = xp_ref[...] + gate * mix(y1p_ref[0], y2p_ref[0], recp_ref[...])

    @pl.when(i == n_p)
    def _():
        os_ref[...] = xs_ref[...] + gates_ref[...] * mix(y1s_ref[0], y2s_ref[0], recs_ref[...])


def _combine(yg, rec, xp, xs, mods, li, mod_s):
    tp, d = xp.shape
    sr = xs.shape[0]
    tm = min(512, SEQ)
    n_p = tp // tm
    last = n_p - 1
    stile = tp // sr
    body = functools.partial(_combine_body, n_p=n_p, tiles_per_batch=SEQ // tm)
    pmap = lambda i: (jnp.minimum(i, last), 0)
    return pl.pallas_call(
        body,
        grid=(n_p + 1,),
        in_specs=[pl.BlockSpec((1, tm, d), lambda i: (0, jnp.minimum(i, last), 0)),
                  pl.BlockSpec((1, tm, d), lambda i: (1, jnp.minimum(i, last), 0)),
                  pl.BlockSpec((1, sr, d), lambda i: (0, stile, 0)),
                  pl.BlockSpec((1, sr, d), lambda i: (1, stile, 0)),
                  pl.BlockSpec((tm, LANES), pmap),
                  pl.BlockSpec((sr, LANES), lambda i: (stile, 0)),
                  pl.BlockSpec((tm, d), pmap),
                  pl.BlockSpec((sr, d), lambda i: (0, 0)),
                  pl.BlockSpec((1, 1, MOD_ROWS, d), lambda i: (li, 5, 0, 0)),
                  pl.BlockSpec((sr, d), lambda i: (0, 0))],
        out_specs=[pl.BlockSpec((tm, d), pmap), pl.BlockSpec((sr, d), lambda i: (0, 0))],
        out_shape=[jax.ShapeDtypeStruct((tp, d), F32), jax.ShapeDtypeStruct((sr, d), F32)],
        compiler_params=_cp(("arbitrary",)),
        name="combine",
    )(yg, yg, yg, yg, rec, rec, xp, xs, mods, mod_s[5])


def _moe(xp, xs, g, mods, li, mod_s, w_rg, b_rg, w_re, b_re, w_in, w_out):
    tp, d = xp.shape
    sr = xs.shape[0]
    n_real = 2 * (tp + (sr // SROWS) * DEC_SEQ)
    p = (n_real + N_EXPERTS * (EXP_TILE - 1) + EXP_TILE - 1) // EXP_TILE * EXP_TILE
    h_all, rec, counts = _router(xp, xs, g, mods, li, mod_s, w_rg, b_rg, w_re, b_re)
    nt = tp + sr
    row = jnp.arange(nt)
    real = jnp.logical_or(row < tp, (row - tp) % SROWS < DEC_SEQ)
    ids = rec[:nt, :6].astype(jnp.int32)
    eid, rank = ids[:, 0:2], ids[:, 4:6]
    count = counts[0, :N_EXPERTS].astype(jnp.int32)
    padded = (count + EXP_TILE - 1) // EXP_TILE * EXP_TILE
    ends = jnp.cumsum(padded)
    start = ends - padded
    pos =jnp.where(real[:, None], start[eid] + rank, p).T.reshape(-1)
    src = (jnp.arange(p, dtype=jnp.int32) % nt).at[pos].set(jnp.tile(row, 2), mode="drop")
    tile_start = jnp.arange(p // EXP_TILE) * EXP_TILE
    tile_expert = jnp.minimum(jnp.sum((ends[None, :] <= tile_start[:, None]).astype(jnp.int32), axis=1),
                              N_EXPERTS - 1)
    n_valid = (ends[-1] // EXP_TILE).astype(jnp.int32)[None]
    x_parts = [jnp.take(h_all, src, axis=0, mode="clip")]
    ys = _experts(x_parts, tile_expert, n_valid, w_in, w_out, li)
    yg = jnp.take(ys, jnp.minimum(pos, p - 1), axis=0, mode="clip").reshape(2, nt, d)
    return _combine(yg, rec, xp, xs, mods, li, mod_s)


def kernel(x_prompt, x_sample, cache_sb_k, cache_sb_v, state_ret, cache_diff_k, cache_diff_v, page_table, c_prompt, c_sample, w_ada, b_ada, norm_mix, norm_ffn, w_in_even, w_out_even, ret_gn_gain, w_in_odd, w_out_odd, qk_gain, diff_lambda, diff_subln, w_router_group, b_router_group, w_router_expert, b_router_expert, w_expert_in, w_expert_out):
    nb, seq, d = x_prompt.shape
    nreq, dec_seq, _ = x_sample.shape
    n_pages = page_table.shape[1]
    past_len = n_pages * PAGE_SIZE
    tp = nb * seq
    sr = nreq * SROWS
    h_sb = cache_sb_k.shape[3]
    h_diff = cache_diff_k.shape[3]

    xp = x_prompt.reshape(tp, d)
    xs = jnp.pad(x_sample, ((0, 0), (0, SROWS - dec_seq), (0, 0))).reshape(sr, d)
    c_all = jnp.concatenate([c_prompt, c_sample, jnp.zeros((MOD_ROWS - nb - nreq, d), F32)], axis=0)
    mods = _ada(c_all, w_ada, b_ada)
    uu = _cumsum_mat(ATT_BLK)
    csk = cache_sb_k.reshape(cache_sb_k.shape[0], cache_sb_k.shape[1], PAGE_SIZE * h_sb, HD_SB)
    csv = cache_sb_v.reshape(csk.shape)
    cdk = cache_diff_k.reshape(cache_diff_k.shape[0], cache_diff_k.shape[1], PAGE_SIZE * h_diff, DV_DIFF)
    cdv = cache_diff_v.reshape(cdk.shape)
    pos_s = past_len + (jnp.arange(sr) % SROWS)

    def sample_rows(a, tail):
        return a.reshape(nreq, SROWS, *tail)[:, :dec_seq]

    sbk_p, sbv_p, sbk_s, sbv_s, ret_p, ret_s = [], [], [], [], [], []
    dk_p, dv_p, dk_s, dv_s = [], [], [], []
    for li in range(DEPTH):
        mod_s = [jnp.repeat(mods[li, k, nb:nb + nreq], SROWS, axis=0) for k in range(ADA_MOD)]
        if li % 2 == 0:
            e = li // 2
            w_in = w_in_even[e].astype(BF16)
            proj_p, proj_s = _inproj(xp, xs, norm_mix[li][None], mods, li, 0, 1, mod_s, w_in, h_sb * HD_SB)
            gain = ret_gn_gain[e][None, :]
            o_sb_p = _sb_prompt(proj_p, uu)
            o_r_p, s_p = _ret_prompt(proj_p, gain)
            o_sb_s = _sb_sample(proj_s, csk, csv, e, page_table, uu)
            o_r_s, s_s = _ret_sample(proj_s, gain, state_ret[e], past_len)
            xp, xs = _outproj((o_sb_p, o_r_p), (o_sb_s, o_r_s), w_out_even[e].astype(BF16),
                              xp, xs, mods, li, 2, mod_s)
            sbk_p.append(proj_p[1])
            sbv_p.append(proj_p[2])
            sbk_s.append(sample_rows(proj_s[1], (h_sb, HD_SB)))
            sbv_s.append(sample_rows(proj_s[2], (h_sb, HD_SB)))
            ret_p.append(s_p)
            ret_s.append(s_s)
        else:
            o = li // 2
            lam_init = 0.8 - 0.6 * math.exp(-0.3 * li)
            w_in = w_in_odd[o].astype(BF16)
            proj_p, proj_s = _inproj(xp, xs, norm_mix[li][None], mods, li, 0, 1, mod_s, w_in, h_diff * DV_DIFF)
            qn_p, kn_p = _diff_prep(proj_p, qk_gain[o], jnp.arange(seq), seq, BF16)
            qn_s, kn_s = _diff_prep(proj_s, qk_gain[o], pos_s, sr, F32)
            o_p = _diff_prompt(qn_p, kn_p, proj_p, diff_lambda[o], diff_subln[o], lam_init)
            o_s = _diff_sample(qn_s, kn_s, proj_s, cdk, cdv, o, page_table, diff_lambda[o],
                               diff_subln[o], lam_init)
            xp, xs = _outproj(o_p, o_s, w_out_odd[o].astype(BF16), xp, xs, mods, li, 2, mod_s)
            dk_p.append(kn_p)
            dv_p.append(proj_p[2])
            dk_s.append(sample_rows(kn_s, (h_diff, DV_DIFF)))
            dv_s.append(sample_rows(proj_s[2], (h_diff, DV_DIFF)))
        xp, xs = _moe(xp, xs, norm_ffn[li][None], mods, li, mod_s, w_router_group[li], b_router_group[li],
                      w_router_expert[li], b_router_expert[li], w_expert_in, w_expert_out)
    def heads(rows_list, n_heads, width):
        return jnp.stack(rows_list).reshape(len(rows_list), nb, seq, n_heads, width)

    return (xp.reshape(nb, seq, d), sample_rows(xs, (d,)),
            heads(sbk_p, h_sb, HD_SB), heads(sbv_p, h_sb, HD_SB), jnp.stack(sbk_s), jnp.stack(sbv_s),
            jnp.stack(ret_p), jnp.stack(ret_s), heads(dk_p, h_diff, DV_DIFF), heads(dv_p, h_diff, DV_DIFF),
            jnp.stack(dk_s), jnp.stack(dv_s))
```
